```python
import math
import jax, jax.numpy as jnp
from jax import lax
import numpy as np

D_MODEL = 1024
BATCH = 2
SEQ = 16384
DEPTH = 2

D_MIX = D_MODEL
D_SGU = D_MIX // 2
D_ATT = D_MIX - D_SGU
SGU_GROUPS = 8
SGU_GROUP_DIM = D_SGU // SGU_GROUPS
SGU_CHUNK = 128
ATT_HEADS = 8
HEAD_DIM = D_ATT // ATT_HEADS
MOBA_BLOCK = 256
MOBA_TOPK = 3
Q_CHUNK = 128
REL_BUCKETS = 32
REL_MAX_DIST = 128
NORM_EPS = 1e-6
LN_EPS = 1e-5
NEG = -1e30
D_IN = 3 * D_SGU + 4 * D_ATT
SPLITS = [D_SGU, 2 * D_SGU, 3 * D_SGU, 3 * D_SGU + D_ATT, 3 * D_SGU + 2 * D_ATT, 3 * D_SGU + 3 * D_ATT]

kernel_name = "hymba_sgu_moba_hybrid"


def rms_norm(x, g):
    xf = x.astype(jnp.float32)
    y = xf * lax.rsqrt(jnp.mean(xf * xf, axis=-1, keepdims=True) + NORM_EPS)
    return (y * g.astype(jnp.float32)).astype(x.dtype)


def layer_norm(x, g, b):
    xf = x.astype(jnp.float32)
    mu = jnp.mean(xf, axis=-1, keepdims=True)
    var = jnp.mean(jnp.square(xf - mu), axis=-1, keepdims=True)
    y = (xf - mu) * lax.rsqrt(var + LN_EPS)
    return (y * g.astype(jnp.float32) + b.astype(jnp.float32)).astype(x.dtype)


def rel_bucket(dist):
    n = jnp.maximum(dist, 0)
    max_exact = REL_BUCKETS // 2
    nf = jnp.maximum(n, 1).astype(jnp.float32)
    large = max_exact + (jnp.log(nf / max_exact) / math.log(REL_MAX_DIST / max_exact)
                         * (REL_BUCKETS - max_exact)).astype(jnp.int32)
    large = jnp.minimum(large, REL_BUCKETS - 1)
    return jnp.where(n < max_exact, n, large)


def sgu_mixer(ua, va, za, ln_g, ln_b, w_s, b_s):
    B, S, _ = ua.shape
    nc = S // SGU_CHUNK
    u = jax.nn.gelu(ua)
    v = layer_norm(jax.nn.gelu(va), ln_g, ln_b)
    v = v.reshape(B, nc, SGU_CHUNK, SGU_GROUPS, SGU_GROUP_DIM)
    causal = jnp.tril(jnp.ones((SGU_CHUNK, SGU_CHUNK), dtype=bool))
    w = jnp.where(causal[None], w_s, jnp.zeros((), w_s.dtype)).astype(v.dtype)
    mixed = jnp.einsum('gts,bcsgd->bctgd', w, v) + b_s.T.astype(v.dtype)[None, None, :, :, None]
    return u * mixed.reshape(B, S, D_SGU) * jax.nn.silu(za)


def moba_mixer(q, k, v, zb, rel_bias):
    B, S, _ = q.shape
    H, dh = ATT_HEADS, HEAD_DIM

    def heads(t):
        return t.reshape(B, S, H, dh).transpose(0, 2, 1, 3)

    q, k, v = heads(q), heads(k), heads(v)
    nb = -(-S // MOBA_BLOCK)
    pad = nb * MOBA_BLOCK - S
    kp = jnp.pad(k, ((0, 0), (0, 0), (0, pad), (0, 0)))
    vp = jnp.pad(v, ((0, 0), (0, 0), (0, pad), (0, 0)))
    k_blocks = kp.reshape(B, H, nb, MOBA_BLOCK, dh)
    v_blocks = vp.reshape(B, H, nb, MOBA_BLOCK, dh)
    k_mean = jnp.mean(k_blocks.astype(jnp.float32), axis=3)

    pos = jnp.arange(S, dtype=jnp.int32)
    own = pos // MOBA_BLOCK
    gate = jnp.einsum('bhsd,bhnd->bhsn', q.astype(jnp.float32), k_mean)
    past = jnp.arange(nb, dtype=jnp.int32)[None, :] < own[:, None]
    gate = jnp.where(past[None, None], gate, NEG)
    ksel = min(MOBA_TOPK, nb)
    _, idx = lax.top_k(gate, ksel)
    valid = jnp.arange(ksel, dtype=jnp.int32)[None, :] < own[:, None]

    nqc = S // Q_CHUNK
    q_ch = q.reshape(B, H, nqc, Q_CHUNK, dh).transpose(2, 0, 1, 3, 4)
    idx_ch = idx.reshape(B, H, nqc, Q_CHUNK, ksel).transpose(2, 0, 1, 3, 4)
    valid_ch = valid.reshape(nqc, Q_CHUNK, ksel)
    scale = HEAD_DIM ** -0.5
    table = rel_bias.T
    b_ix = jnp.arange(B)[:, None, None, None]
    h_ix = jnp.arange(H)[None, :, None, None]
    blk_ar = jnp.arange(MOBA_BLOCK, dtype=jnp.int32)

    def chunk(args):
        c, q_c, idx_c, valid_c = args
        q_pos = c * Q_CHUNK + jnp.arange(Q_CHUNK, dtype=jnp.int32)
        kb = (c * Q_CHUNK) // MOBA_BLOCK
        k_own = lax.dynamic_slice_in_dim(kp, kb * MOBA_BLOCK, MOBA_BLOCK, axis=2)
        v_own = lax.dynamic_slice_in_dim(vp, kb * MOBA_BLOCK, MOBA_BLOCK, axis=2)
        d_own = q_pos[:, None] - (kb * MOBA_BLOCK + blk_ar)[None, :]
        s_own = (jnp.einsum('bhqd,bhkd->bhqk', q_c, k_own).astype(jnp.float32) * scale
                 + table[:, rel_bucket(d_own)][None].astype(jnp.float32))
        s_own = jnp.where((d_own >= 0)[None, None], s_own, NEG)
        k_sel = k_blocks[b_ix, h_ix, idx_c]
        v_sel = v_blocks[b_ix, h_ix, idx_c]
        k_pos_sel = idx_c[..., None] * MOBA_BLOCK + blk_ar
        d_sel = q_pos[None, None, :, None, None] - k_pos_sel
        bias_sel = table[h_ix[..., None], rel_bucket(d_sel)].astype(jnp.float32)
        s_sel = jnp.einsum('bhqd,bhqjkd->bhqjk', q_c, k_sel).astype(jnp.float32) * scale + bias_sel
        s_sel = jnp.where(valid_c[None, None, :, :, None], s_sel, NEG)
        logits = jnp.concatenate([s_own, s_sel.reshape(B, H, Q_CHUNK, ksel * MOBA_BLOCK)], axis=-1)
        p = jax.nn.softmax(logits, axis=-1).astype(v_own.dtype)
        p_own = p[..., :MOBA_BLOCK]
        p_sel = p[..., MOBA_BLOCK:].reshape(B, H, Q_CHUNK, ksel, MOBA_BLOCK)
        return (jnp.einsum('bhqk,bhkd->bhqd', p_own, v_own)
                + jnp.einsum('bhqjk,bhqjkd->bhqd', p_sel, v_sel))

    out = lax.map(chunk, (jnp.arange(nqc, dtype=jnp.int32), q_ch, idx_ch, valid_ch))
    out = out.transpose(1, 2, 0, 3, 4).reshape(B, H, S, dh).transpose(0, 2, 1, 3).reshape(B, S, D_ATT)
    return out * jax.nn.silu(zb)


def hybrid_layer(x, norm_g, w_in, ln_g, ln_b, w_s, b_s, w_out, rel_bias):
    h = rms_norm(x, norm_g)
    proj = h @ w_in
    ua, va, za, q, k, v, zb = jnp.split(proj, SPLITS, axis=-1)
    y_a = sgu_mixer(ua, va, za, ln_g, ln_b, w_s, b_s)
    y_b = moba_mixer(q, k, v, zb, rel_bias)
    y = jnp.concatenate([y_a, y_b], axis=-1) @ w_out
    return x + y


def setup_inputs(seed: int = 0) -> dict:
    key = jax.random.key(seed)
    ks = jax.random.split(key, 10)
    f32 = jnp.float32
    x = jax.random.normal(ks[0], (BATCH, SEQ, D_MODEL), f32)
    norm_g = 1.0 + 0.02 * jax.random.normal(ks[1], (DEPTH, D_MODEL), f32)
    w_in = jax.random.normal(ks[2], (DEPTH, D_MODEL, D_IN), f32) * D_MODEL ** -0.5
    sgu_ln_g = 1.0 + 0.02 * jax.random.normal(ks[3], (DEPTH, D_SGU), f32)
    sgu_ln_b = 0.02 * jax.random.normal(ks[4], (DEPTH, D_SGU), f32)
    sgu_w = jax.random.normal(ks[5], (DEPTH, SGU_GROUPS, SGU_CHUNK, SGU_CHUNK), f32) * SGU_CHUNK ** -0.5
    sgu_b = 1.0 + 0.02 * jax.random.normal(ks[6], (DEPTH, SGU_GROUPS, SGU_CHUNK), f32)
    w_out = jax.random.normal(ks[7], (DEPTH, D_MIX, D_MODEL), f32) * D_MIX ** -0.5
    rel_bias = 0.1 * jax.random.normal(ks[8], (REL_BUCKETS, ATT_HEADS), f32)
    final_g = 1.0 + 0.02 * jax.random.normal(ks[9], (D_MODEL,), f32)
    return {"x": x, "norm_g": norm_g, "w_in": w_in, "sgu_ln_g": sgu_ln_g, "sgu_ln_b": sgu_ln_b,
            "sgu_w": sgu_w, "sgu_b": sgu_b, "w_out": w_out, "rel_bias": rel_bias, "final_g": final_g}


def reference(x, norm_g, w_in, sgu_ln_g, sgu_ln_b, sgu_w, sgu_b, w_out, rel_bias, final_g):
    for l in range(DEPTH):
        x = hybrid_layer(x, norm_g[l], w_in[l], sgu_ln_g[l], sgu_ln_b[l], sgu_w[l], sgu_b[l],
                         w_out[l], rel_bias)
    return rms_norm(x, final_g)
```

```python
import functools
import math

import jax
import jax.numpy as jnp
from jax import lax
from jax.experimental import pallas as pl
from jax.experimental.pallas import tpu as pltpu

D_MODEL = 1024
D_SGU = 512
D_ATT = 512
SGU_GROUPS = 8
SGU_GROUP_DIM = D_SGU // SGU_GROUPS
SGU_CHUNK = 128
ATT_HEADS = 8
HEAD_DIM = D_ATT // ATT_HEADS
MOBA_BLOCK = 256
MOBA_TOPK = 3
REL_BUCKETS = 32
REL_MAX_DIST = 128
NORM_EPS = 1e-6
LN_EPS = 1e-5
NEG = -1e30
D_IN = 3 * D_SGU + 4 * D_ATT

LANES = 128
HEADS_PER_STEP = LANES // HEAD_DIM
HEAD_PAIRS = ATT_HEADS // HEADS_PER_STEP
ROWS_IN = 256
ROWS_OUT = 512
VMEM_LIMIT = 48 * 1024 * 1024

_BF16 = jnp.bfloat16
_F32 = jnp.float32


def _dot(a, b):
    return jnp.dot(a, b, preferred_element_type=_F32)


def _dot_nt(a, b):
    return lax.dot_general(a, b, (((1,), (1,)), ((), ())), preferred_element_type=_F32)


def _in_proj_kernel(x_ref, g_ref, w_ref, lng_ref, lnb_ref, ws_ref, bs_ref,
                    ya_ref, q_ref, k_ref, v_ref, zb_ref, ksum_ref):
    x = x_ref[...]
    ms = jnp.mean(x * x, axis=-1, keepdims=True)
    h = (x * lax.rsqrt(ms + NORM_EPS) * g_ref[...]).astype(_BF16)

    def proj(seg):
        return _dot(h, w_ref[:, seg * D_SGU:(seg + 1) * D_SGU])

    gv = jax.nn.gelu(proj(1))
    mu = jnp.mean(gv, axis=-1, keepdims=True)
    var = jnp.mean(jnp.square(gv - mu), axis=-1, keepdims=True)
    vln = (gv - mu) * lax.rsqrt(var + LN_EPS) * lng_ref[...] + lnb_ref[...]

    gate = jax.nn.gelu(proj(0)) * jax.nn.silu(proj(2))

    lane = lax.broadcasted_iota(jnp.int32, (SGU_CHUNK, LANES), 1)
    low = lane < SGU_GROUP_DIM
    for c in range(ROWS_IN // SGU_CHUNK):
        rows = slice(c * SGU_CHUNK, (c + 1) * SGU_CHUNK)
        for p in range(D_SGU // LANES):
            cols = slice(p * LANES, (p + 1) * LANES)
            vp = vln[rows, cols]
            rhs = jnp.concatenate([jnp.where(low, vp, 0.0).astype(_BF16),
                                   jnp.where(low, 0.0, vp).astype(_BF16)], axis=0)
            mixed = _dot(ws_ref[p], rhs) + bs_ref[:, cols]
            ya_ref[rows, cols] = (gate[rows, cols] * mixed).astype(_BF16)

    q_ref[...] = (proj(3) * (HEAD_DIM ** -0.5)).astype(_BF16)
    k = proj(4)
    k_ref[...] = k.astype(_BF16)
    ksum_ref[...] = jnp.sum(k, axis=0, keepdims=True)[None]
    v_ref[...] = proj(5).astype(_BF16)
    zb_ref[...] = proj(6)


def _in_proj(x2d, g, w_bf, lng, lnb, ws_pairs, bs_full):
    n = x2d.shape[0]
    nblk = n // ROWS_IN
    row = lambda i: (i, 0)
    const2 = lambda i: (0, 0)
    seg_spec = pl.BlockSpec((ROWS_IN, D_ATT), row)
    return pl.pallas_call(
        _in_proj_kernel,
        grid=(nblk,),
        in_specs=[
            pl.BlockSpec((ROWS_IN, D_MODEL), row),
            pl.BlockSpec((1, D_MODEL), const2),
            pl.BlockSpec((D_MODEL, D_IN), const2),
            pl.BlockSpec((1, D_SGU), const2),
            pl.BlockSpec((1, D_SGU), const2),
            pl.BlockSpec((D_SGU // LANES, SGU_CHUNK, 2 * SGU_CHUNK), lambda i: (0, 0, 0)),
            pl.BlockSpec((SGU_CHUNK, D_SGU), const2),
        ],
        out_specs=[
            seg_spec, seg_spec, seg_spec, seg_spec, seg_spec,
            pl.BlockSpec((1, 1, D_ATT), lambda i: (i, 0, 0)),
        ],
        out_shape=[
            jax.ShapeDtypeStruct((n, D_SGU), _BF16),
            jax.ShapeDtypeStruct((n, D_ATT), _BF16),
            jax.ShapeDtypeStruct((n, D_ATT), _BF16),
            jax.ShapeDtypeStruct((n, D_ATT), _BF16),
            jax.ShapeDtypeStruct((n, D_ATT), _F32),
            jax.ShapeDtypeStruct((nblk, 1, D_ATT), _F32),
        ],
        compiler_params=pltpu.CompilerParams(
            dimension_semantics=("arbitrary",), vmem_limit_bytes=VMEM_LIMIT),
        name="in_proj",
    )(x2d, g, w_bf, lng, lnb, ws_pairs, bs_full)


def _moba_kernel(q_ref, k_ref, v_ref, kmt_ref, zb_ref, bias_ref, far_ref,
                 yb_ref, qaug_ref, m_ref, l_ref, acc_ref):
    i = pl.program_id(2)
    blk = MOBA_BLOCK
    lane = lax.broadcasted_iota(jnp.int32, (blk, LANES), 1)
    lane_f = lane.astype(_F32)

    q = q_ref[...]
    qf = q.astype(_F32)
    kmt = kmt_ref[...]
    past = lane < i
    for hh in range(HEADS_PER_STEP):
        in_head = (lane >= hh * HEAD_DIM) & (lane < (hh + 1) * HEAD_DIM)
        qh = jnp.where(in_head, qf, 0.0)
        g = jnp.dot(qh, kmt, preferred_element_type=_F32, precision=lax.Precision.HIGHEST)
        g = jnp.where(past, g, NEG)
        sel = jnp.zeros((blk, LANES), jnp.bool_)
        for r in range(MOBA_TOPK):
            top = jnp.max(g, axis=-1, keepdims=True)
            idx = jnp.min(jnp.where(g == top, lane_f, float(LANES)), axis=-1, keepdims=True)
            pick = lane_f == idx
            sel = sel | (pick & (jnp.full_like(lane, r) < i))
            g = jnp.where(pick, -3e38, g)
        rows = slice(hh * blk, (hh + 1) * blk)
        qaug_ref[rows, 0:LANES] = qh.astype(_BF16)
        qaug_ref[rows, LANES:2 * LANES] = jnp.where(sel, far_ref[hh], NEG).astype(_BF16)

    def scores(j, one_hot_on):
        kj = k_ref[pl.ds(pl.multiple_of(j * blk, blk), blk), :]
        if one_hot_on:
            oh = jnp.where(lane == j, 1.0, 0.0).astype(_BF16)
        else:
            oh = jnp.zeros((blk, LANES), _BF16)
        return _dot_nt(qaug_ref[...], jnp.concatenate([kj, oh], axis=1))

    def add_tile(s, t):
        return s + jnp.concatenate([bias_ref[0, t], bias_ref[1, t]], axis=0)

    def pv(p, j):
        vj = v_ref[pl.ds(pl.multiple_of(j * blk, blk), blk), :]
        return _dot(p.astype(_BF16), vj)

    s = add_tile(scores(i, False), 0)
    m0 = jnp.max(s, axis=-1, keepdims=True)
    p = jnp.exp(s - m0)
    m_ref[...] = jnp.broadcast_to(m0, m_ref.shape)
    l_ref[...] = jnp.broadcast_to(jnp.sum(p, axis=-1, keepdims=True), l_ref.shape)
    acc_ref[...] = pv(p, i)

    def online_update(s, j):
        m_prev = m_ref[...]
        m_new = jnp.maximum(m_prev, jnp.max(s, axis=-1, keepdims=True))
        alpha = jnp.exp(m_prev - m_new)
        p = jnp.exp(s - jnp.concatenate([m_new, m_new], axis=1))
        l_ref[...] = alpha * l_ref[...] + jnp.sum(p, axis=-1, keepdims=True)
        acc_ref[...] = alpha * acc_ref[...] + pv(p, j)
        m_ref[...] = m_new

    @pl.when(i >= 1)
    def _():
        online_update(add_tile(scores(i - 1, True), 1), i - 1)

    def far_step(j, carry):
        online_update(scores(j, True), j)
        return carry

    lax.fori_loop(0, jnp.maximum(i - 1, 0), far_step, 0)

    o = acc_ref[...] / l_ref[...]
    o = jnp.where(lane < HEAD_DIM, o[0:blk], o[blk:2 * blk])
    yb_ref[...] = (o * jax.nn.silu(zb_ref[...])).astype(_BF16)


def _moba(q, k, v, kmt, zb, bias_tiles, far_rows, batch, seq):
    n = q.shape[0]
    nq = seq // MOBA_BLOCK
    tile = pl.BlockSpec((MOBA_BLOCK, LANES), lambda b, hp, i: (b * nq + i, hp))
    whole = pl.BlockSpec((seq, LANES), lambda b, hp, i: (b, hp))
    stack = HEADS_PER_STEP * MOBA_BLOCK
    return pl.pallas_call(
        _moba_kernel,
        grid=(batch, HEAD_PAIRS, nq),
        in_specs=[
            tile, whole, whole,
            pl.BlockSpec((None, None, LANES, LANES), lambda b, hp, i: (b, hp, 0, 0)),
            tile,
            pl.BlockSpec((HEADS_PER_STEP, 2, MOBA_BLOCK, MOBA_BLOCK), lambda b, hp, i: (hp, 0, 0, 0)),
            pl.BlockSpec((HEADS_PER_STEP, 1, LANES), lambda b, hp, i: (hp, 0, 0)),
        ],
        out_specs=tile,
        out_shape=jax.ShapeDtypeStruct((n, D_ATT), _BF16),
        scratch_shapes=[
            pltpu.VMEM((stack, 2 * LANES), _BF16),
            pltpu.VMEM((stack, LANES), _F32),
            pltpu.VMEM((stack, LANES), _F32),
            pltpu.VMEM((stack, LANES), _F32),
        ],
        compiler_params=pltpu.CompilerParams(
            dimension_semantics=("arbitrary", "arbitrary", "arbitrary"),
            vmem_limit_bytes=VMEM_LIMIT),
        name="moba",
    )(q, k, v, kmt, zb, bias_tiles, far_rows)


def _out_proj_kernel(ya_ref, yb_ref, x_ref, w_ref, g_ref, o_ref, *, final):
    y = (x_ref[...] + _dot(ya_ref[...], w_ref[0:D_SGU, :])
         + _dot(yb_ref[...], w_ref[D_SGU:D_SGU + D_ATT, :]))
    if final:
        ms = jnp.mean(y * y, axis=-1, keepdims=True)
        y = y * lax.rsqrt(ms + NORM_EPS) * g_ref[...]
    o_ref[...] = y


def _out_proj(ya, yb, x2d, w_bf, final_g, final):
    n = x2d.shape[0]
    row = lambda i: (i, 0)
    const2 = lambda i: (0, 0)
    return pl.pallas_call(
        functools.partial(_out_proj_kernel, final=final),
        grid=(n // ROWS_OUT,),
        in_specs=[
            pl.BlockSpec((ROWS_OUT, D_SGU), row),
            pl.BlockSpec((ROWS_OUT, D_ATT), row),
            pl.BlockSpec((ROWS_OUT, D_MODEL), row),
            pl.BlockSpec((D_SGU + D_ATT, D_MODEL), const2),
            pl.BlockSpec((1, D_MODEL), const2),
        ],
        out_specs=pl.BlockSpec((ROWS_OUT, D_MODEL), row),
        out_shape=jax.ShapeDtypeStruct((n, D_MODEL), _F32),
        compiler_params=pltpu.CompilerParams(
            dimension_semantics=("arbitrary",), vmem_limit_bytes=VMEM_LIMIT),
        name="out_proj",
    )(ya, yb, x2d, w_bf, final_g)


def _rel_bucket(dist):
    n = jnp.maximum(dist, 0)
    max_exact = REL_BUCKETS // 2
    nf = jnp.maximum(n, 1).astype(_F32)
    large = max_exact + (jnp.log(nf / max_exact) / math.log(REL_MAX_DIST / max_exact)
                         * (REL_BUCKETS - max_exact)).astype(jnp.int32)
    large = jnp.minimum(large, REL_BUCKETS - 1)
    return jnp.where(n < max_exact, n, large)


def _bias_tables(rel_bias):
    table = rel_bias.T.astype(_F32)
    ar = jnp.arange(MOBA_BLOCK, dtype=jnp.int32)
    d_own = ar[:, None] - ar[None, :]
    far = table[:, REL_BUCKETS - 1]
    own = jnp.where((d_own >= 0)[None], table[:, _rel_bucket(d_own)], NEG)
    prev = table[:, _rel_bucket(d_own + MOBA_BLOCK)] - far[:, None, None]
    tiles = jnp.stack([own, prev], axis=1)
    far_rows = jnp.broadcast_to(far[:, None, None], (ATT_HEADS, 1, LANES))
    return tiles, far_rows


def _sgu_tables(w_s, b_s):
    causal = jnp.tril(jnp.ones((SGU_CHUNK, SGU_CHUNK), dtype=bool))
    w = jnp.where(causal[None], w_s, 0.0).astype(_BF16)
    pairs = w.reshape(SGU_GROUPS // 2, 2, SGU_CHUNK, SGU_CHUNK).transpose(0, 2, 1, 3)
    pairs = pairs.reshape(SGU_GROUPS // 2, SGU_CHUNK, 2 * SGU_CHUNK)
    bs_full = jnp.repeat(b_s.T.astype(_F32), SGU_GROUP_DIM, axis=1)
    return pairs, bs_full


def kernel(x, norm_g, w_in, sgu_ln_g, sgu_ln_b, sgu_w, sgu_b, w_out, rel_bias, final_g):
    batch, seq, _ = x.shape
    depth = norm_g.shape[0]
    nq = seq // MOBA_BLOCK
    assert seq % MOBA_BLOCK == 0 and nq <= LANES
    x2d = x.reshape(batch * seq, D_MODEL)
    bias_tiles, far_rows = _bias_tables(rel_bias)
    fg = final_g.reshape(1, D_MODEL)
    for l in range(depth):
        ws_pairs, bs_full = _sgu_tables(sgu_w[l], sgu_b[l])
        ya, q, k, v, zb, ksum = _in_proj(
            x2d, norm_g[l].reshape(1, D_MODEL), w_in[l].astype(_BF16),
            sgu_ln_g[l].reshape(1, D_SGU), sgu_ln_b[l].reshape(1, D_SGU), ws_pairs, bs_full)
        kmean = ksum.reshape(batch, nq, HEAD_PAIRS, LANES) * (1.0 / MOBA_BLOCK)
        kmt = jnp.pad(kmean.transpose(0, 2, 3, 1), ((0, 0), (0, 0), (0, 0), (0, LANES - nq)))
        yb = _moba(q, k, v, kmt, zb, bias_tiles, far_rows, batch, seq)
        x2d = _out_proj(ya, yb, x2d, w_out[l].astype(_BF16), fg, final=(l == depth - 1))
    return x2d.reshape(batch, seq, D_MODEL)
```

```python
import functools
import math

import jax
import jax.numpy as jnp
from jax import lax
from jax.experimental import pallas as pl
from jax.experimental.pallas import tpu as pltpu

D_MODEL = 1024
D_SGU = 512
D_ATT = 512
SGU_GROUPS = 8
SGU_GROUP_DIM = D_SGU // SGU_GROUPS
SGU_CHUNK = 128
ATT_HEADS = 8
HEAD_DIM = D_ATT // ATT_HEADS
MOBA_BLOCK = 256
MOBA_TOPK = 3
REL_BUCKETS = 32
REL_MAX_DIST = 128
NORM_EPS = 1e-6
LN_EPS = 1e-5
NEG = -1e30
LOG2E = math.log2(math.e)
D_IN = 3 * D_SGU + 4 * D_ATT
V_SEG = 5

LANES = 128
BF16_ROWS = 16
HEADS_PER_STEP = LANES // HEAD_DIM
HEAD_PAIRS = ATT_HEADS // HEADS_PER_STEP
VT_ROWS = HEAD_DIM + BF16_ROWS
FAR_GROUP = 4
FAR_GROUP_LOG2 = 2
DEAD_LANE = LANES - 1
ROWS_IN = 256
ROWS_OUT = 512
VMEM_LIMIT = 48 * 1024 * 1024

_BF16 = jnp.bfloat16
_F32 = jnp.float32


def _dot(a, b):
    return jnp.dot(a, b, preferred_element_type=_F32)


def _dot_nt(a, b):
    return lax.dot_general(a, b, (((1,), (1,)), ((), ())), preferred_element_type=_F32)


def _in_proj_kernel(x_ref, g_ref, w_ref, wvt_ref, lng_ref, lnb_ref, ws_ref, bs_ref,
                    ya_ref, q_ref, k_ref, vt_ref, zb_ref, ksum_ref):
    x = x_ref[...]
    ms = jnp.mean(x * x, axis=-1, keepdims=True)
    h = (x * lax.rsqrt(ms + NORM_EPS) * g_ref[...]).astype(_BF16)

    def proj(seg):
        return _dot(h, w_ref[:, seg * D_SGU:(seg + 1) * D_SGU])

    gv = jax.nn.gelu(proj(1))
    mu = jnp.mean(gv, axis=-1, keepdims=True)
    var = jnp.mean(jnp.square(gv - mu), axis=-1, keepdims=True)
    vln = (gv - mu) * lax.rsqrt(var + LN_EPS) * lng_ref[...] + lnb_ref[...]

    gate = jax.nn.gelu(proj(0)) * jax.nn.silu(proj(2))

    lane = lax.broadcasted_iota(jnp.int32, (SGU_CHUNK, LANES), 1)
    low = lane < SGU_GROUP_DIM
    for c in range(ROWS_IN // SGU_CHUNK):
        rows = slice(c * SGU_CHUNK, (c + 1) * SGU_CHUNK)
        for p in range(D_SGU // LANES):
            cols = slice(p * LANES, (p + 1) * LANES)
            vp = vln[rows, cols]
            rhs = jnp.concatenate([jnp.where(low, vp, 0.0).astype(_BF16),
                                   jnp.where(low, 0.0, vp).astype(_BF16)], axis=0)
            mixed = _dot(ws_ref[p], rhs) + bs_ref[:, cols]
            ya_ref[rows, cols] = (gate[rows, cols] * mixed).astype(_BF16)

    q_ref[...] = (proj(3) * (HEAD_DIM ** -0.5 * LOG2E)).astype(_BF16)
    k = proj(4)
    k_ref[...] = k.astype(_BF16)
    ksum_ref[...] = jnp.sum(k, axis=0, keepdims=True)[None]
    zb_ref[...] = proj(6)

    vt = _dot_nt(wvt_ref[...], h).astype(_BF16)
    ones = jnp.ones((BF16_ROWS, ROWS_IN), _BF16)
    for hd in range(ATT_HEADS):
        vt_ref[0, hd * VT_ROWS:hd * VT_ROWS + HEAD_DIM, :] = vt[hd * HEAD_DIM:(hd + 1) * HEAD_DIM, :]
        vt_ref[0, hd * VT_ROWS + HEAD_DIM:(hd + 1) * VT_ROWS, :] = ones


def _in_proj(x2d, g, w_bf, wvt_bf, lng, lnb, ws_pairs, bs_full):
    n = x2d.shape[0]
    nblk = n // ROWS_IN
    row = lambda i: (i, 0)
    const2 = lambda i: (0, 0)
    seg_spec = pl.BlockSpec((ROWS_IN, D_ATT), row)
    return pl.pallas_call(
        _in_proj_kernel,
        grid=(nblk,),
        in_specs=[
            pl.BlockSpec((ROWS_IN, D_MODEL), row),
            pl.BlockSpec((1, D_MODEL), const2),
            pl.BlockSpec((D_MODEL, D_IN), const2),
            pl.BlockSpec((D_ATT, D_MODEL), const2),
            pl.BlockSpec((1, D_SGU), const2),
            pl.BlockSpec((1, D_SGU), const2),
            pl.BlockSpec((D_SGU // LANES, SGU_CHUNK, 2 * SGU_CHUNK), lambda i: (0, 0, 0)),
            pl.BlockSpec((SGU_CHUNK, D_SGU), const2),
        ],
        out_specs=[
            seg_spec, seg_spec, seg_spec,
            pl.BlockSpec((1, ATT_HEADS * VT_ROWS, ROWS_IN), lambda i: (i, 0, 0)),
            seg_spec,
            pl.BlockSpec((1, 1, D_ATT), lambda i: (i, 0, 0)),
        ],
        out_shape=[
            jax.ShapeDtypeStruct((n, D_SGU), _BF16),
            jax.ShapeDtypeStruct((n, D_ATT), _BF16),
            jax.ShapeDtypeStruct((n, D_ATT), _BF16),
            jax.ShapeDtypeStruct((nblk, ATT_HEADS * VT_ROWS, ROWS_IN), _BF16),
            jax.ShapeDtypeStruct((n, D_ATT), _F32),
            jax.ShapeDtypeStruct((nblk, 1, D_ATT), _F32),
        ],
        compiler_params=pltpu.CompilerParams(
            dimension_semantics=("arbitrary",), vmem_limit_bytes=VMEM_LIMIT),
        name="in_proj",
    )(x2d, g, w_bf, wvt_bf, lng, lnb, ws_pairs, bs_full)


def _toeplitz(row_ref, hh, t):
    wide = jnp.broadcast_to(row_ref[hh, t], (MOBA_BLOCK, 2 * MOBA_BLOCK))
    return pltpu.roll(wide, 0, 1, stride=1, stride_axis=0)[:, 0:MOBA_BLOCK]


def _moba_kernel(q_ref, k_ref, ohk_ref, vt_ref, kmt_ref, zb_ref, brow_ref, far_ref,
                 yb_ref, qaug_ref, qfar_ref, bias_ref, m_ref, acc_ref):
    i = pl.program_id(2)
    blk = MOBA_BLOCK
    lane = lax.broadcasted_iota(jnp.int32, (blk, LANES), 1)
    lane_f = lane.astype(_F32)

    @pl.when(i == 0)
    def _():
        for hh in range(HEADS_PER_STEP):
            for t in range(2):
                bias_ref[t, :, hh * blk:(hh + 1) * blk] = _toeplitz(brow_ref, hh, t)

    q = q_ref[...]
    qf = q.astype(_F32)
    kmt = kmt_ref[...]
    past = lane < i
    for hh in range(HEADS_PER_STEP):
        in_head = (lane >= hh * HEAD_DIM) & (lane < (hh + 1) * HEAD_DIM)
        qh = jnp.where(in_head, qf, 0.0)
        g = jnp.dot(qh, kmt, preferred_element_type=_F32, precision=lax.Precision.HIGHEST)
        g = jnp.where(past, g, NEG)
        sel = jnp.zeros((blk, LANES), jnp.bool_)
        for r in range(MOBA_TOPK):
            top = jnp.max(g, axis=-1, keepdims=True)
            idx = jnp.min(jnp.where(g == top, lane_f, float(LANES)), axis=-1, keepdims=True)
            pick = lane_f == idx
            sel = sel | (pick & (jnp.full_like(lane, r) < i))
            g = jnp.where(pick, -3e38, g)
        rows = slice(hh * blk, (hh + 1) * blk)
        qh16 = qh.astype(_BF16)
        qaug_ref[rows, 0:LANES] = qh16
        qfar_ref[rows, 0:LANES] = qh16
        selb = jnp.where(sel, far_ref[hh], NEG)
        qaug_ref[rows, LANES:2 * LANES] = selb.astype(_BF16)
        qfar_ref[rows, LANES:2 * LANES] = jnp.where(lane == i - 1, NEG, selb).astype(_BF16)

    def key_rows(j0, nblk):
        return pl.ds(pl.multiple_of(j0 * blk, blk), nblk * blk)

    def vt_rows(hh):
        return slice(hh * VT_ROWS, (hh + 1) * VT_ROWS)

    def pv(p, j0, nblk, hh):
        cols = slice(hh * blk, (hh + 1) * blk)
        o = _dot(vt_ref[j0, vt_rows(hh), :], p[0:blk, cols])
        for g_ in range(1, nblk):
            o = o + _dot(vt_ref[j0 + g_, vt_rows(hh), :], p[g_ * blk:(g_ + 1) * blk, cols])
        return o

    ka = jnp.concatenate([k_ref[key_rows(i, 1), :], jnp.zeros((blk, LANES), _BF16)], axis=1)
    s = _dot_nt(ka, qaug_ref[...]) + bias_ref[0]
    m0 = jnp.max(s, axis=0, keepdims=True)
    p = jnp.exp2(s - m0).astype(_BF16)
    m_ref[...] = m0
    for hh in range(HEADS_PER_STEP):
        acc_ref[hh] = pv(p, i, 1, hh)

    def online_update(s, j0, nblk):
        m_prev = m_ref[...]
        m_new = jnp.maximum(m_prev, jnp.max(s, axis=0, keepdims=True))
        alpha = jnp.exp2(m_prev - m_new)
        p = jnp.exp2(s - m_new).astype(_BF16)
        for hh in range(HEADS_PER_STEP):
            acc_ref[hh] = alpha[:, hh * blk:(hh + 1) * blk] * acc_ref[hh] + pv(p, j0, nblk, hh)
        m_ref[...] = m_new

    @pl.when(i >= 1)
    def _():
        rows = key_rows(i - 1, 1)
        ka = jnp.concatenate([k_ref[rows, :], ohk_ref[rows, :]], axis=1)
        online_update(_dot_nt(ka, qaug_ref[...]) + bias_ref[1], i - 1, 1)

    def far_step(g_, carry):
        j0 = g_ * FAR_GROUP
        rows = key_rows(j0, FAR_GROUP)
        ka = jnp.concatenate([k_ref[rows, :], ohk_ref[rows, :]], axis=1)
        online_update(_dot_nt(ka, qfar_ref[...]), j0, FAR_GROUP)
        return carry

    n_far = jnp.maximum(i - 1, 0)
    lax.fori_loop(0, lax.shift_right_logical(n_far + (FAR_GROUP - 1), FAR_GROUP_LOG2), far_step, 0)

    o_t = jnp.concatenate(
        [acc_ref[hh, 0:HEAD_DIM, :] / acc_ref[hh, HEAD_DIM:HEAD_DIM + 1, :]
         for hh in range(HEADS_PER_STEP)], axis=0)
    yb_ref[...] = (o_t.T * jax.nn.silu(zb_ref[...])).astype(_BF16)


def _moba(q, k, ohk, vt, kmt, zb, bias_rows, far_rows, batch, seq):
    n = q.shape[0]
    nq = seq // MOBA_BLOCK
    tile = pl.BlockSpec((MOBA_BLOCK, LANES), lambda b, hp, i: (b * nq + i, hp))
    stack = HEADS_PER_STEP * MOBA_BLOCK
    return pl.pallas_call(
        _moba_kernel,
        grid=(batch, HEAD_PAIRS, nq),
        in_specs=[
            tile,
            pl.BlockSpec((seq, LANES), lambda b, hp, i: (b, hp)),
            pl.BlockSpec((seq, LANES), lambda b, hp, i: (0, 0)),
            pl.BlockSpec((nq, HEADS_PER_STEP * VT_ROWS, MOBA_BLOCK), lambda b, hp, i: (b, hp, 0)),
            pl.BlockSpec((None, None, LANES, LANES), lambda b, hp, i: (b, hp, 0, 0)),
            tile,
            pl.BlockSpec((HEADS_PER_STEP, 2, 1, 2 * MOBA_BLOCK), lambda b, hp, i: (hp, 0, 0, 0)),
            pl.BlockSpec((HEADS_PER_STEP, 1, LANES), lambda b, hp, i: (hp, 0, 0)),
        ],
        out_specs=tile,
        out_shape=jax.ShapeDtypeStruct((n, D_ATT), _BF16),
        scratch_shapes=[
            pltpu.VMEM((stack, 2 * LANES), _BF16),
            pltpu.VMEM((stack, 2 * LANES), _BF16),
            pltpu.VMEM((2, MOBA_BLOCK, stack), _F32),
            pltpu.VMEM((1, stack), _F32),
            pltpu.VMEM((HEADS_PER_STEP, VT_ROWS, MOBA_BLOCK), _F32),
        ],
        compiler_params=pltpu.CompilerParams(
            dimension_semantics=("arbitrary", "arbitrary", "arbitrary"),
            vmem_limit_bytes=VMEM_LIMIT),
        name="moba",
    )(q, k, ohk, vt, kmt, zb, bias_rows, far_rows)


def _out_proj_kernel(ya_ref, yb_ref, x_ref, w_ref, g_ref, o_ref, *, final):
    y = (x_ref[...] + _dot(ya_ref[...], w_ref[0:D_SGU, :])
         + _dot(yb_ref[...], w_ref[D_SGU:D_SGU + D_ATT, :]))
    if final:
        ms = jnp.mean(y * y, axis=-1, keepdims=True)
        y = y * lax.rsqrt(ms + NORM_EPS) * g_ref[...]
    o_ref[...] = y


def _out_proj(ya, yb, x2d, w_bf, final_g, final):
    n = x2d.shape[0]
    row = lambda i: (i, 0)
    const2 = lambda i: (0, 0)
    return pl.pallas_call(
        functools.partial(_out_proj_kernel, final=final),
        grid=(n // ROWS_OUT,),
        in_specs=[
            pl.BlockSpec((ROWS_OUT, D_SGU), row),
            pl.BlockSpec((ROWS_OUT, D_ATT), row),
            pl.BlockSpec((ROWS_OUT, D_MODEL), row),
            pl.BlockSpec((D_SGU + D_ATT, D_MODEL), const2),
            pl.BlockSpec((1, D_MODEL), const2),
        ],
        out_specs=pl.BlockSpec((ROWS_OUT, D_MODEL), row),
        out_shape=jax.ShapeDtypeStruct((n, D_MODEL), _F32),
        compiler_params=pltpu.CompilerParams(
            dimension_semantics=("arbitrary",), vmem_limit_bytes=VMEM_LIMIT),
        name="out_proj",
    )(ya, yb, x2d, w_bf, final_g)


def _rel_bucket(dist):
    n = jnp.maximum(dist, 0)
    max_exact = REL_BUCKETS // 2
    nf = jnp.maximum(n, 1).astype(_F32)
    large = max_exact + (jnp.log(nf / max_exact) / math.log(REL_MAX_DIST / max_exact)
                         * (REL_BUCKETS - max_exact)).astype(jnp.int32)
    large = jnp.minimum(large, REL_BUCKETS - 1)
    return jnp.where(n < max_exact, n, large)


def _bias_tables(rel_bias):
    table = rel_bias.T.astype(_F32) * LOG2E
    period = 2 * MOBA_BLOCK
    u = jnp.arange(period, dtype=jnp.int32)
    far = table[:, REL_BUCKETS - 1]
    own = jnp.where((u < MOBA_BLOCK)[None], table[:, _rel_bucket(u)], NEG)
    prev = table[:, _rel_bucket((u + MOBA_BLOCK) % period)] - far[:, None]
    rows = jnp.stack([own, prev], axis=1)[:, :, None, :]
    far_rows = jnp.broadcast_to(far[:, None, None], (ATT_HEADS, 1, LANES))
    return rows, far_rows


def _sgu_tables(w_s, b_s):
    causal = jnp.tril(jnp.ones((SGU_CHUNK, SGU_CHUNK), dtype=bool))
    w = jnp.where(causal[None], w_s, 0.0).astype(_BF16)
    pairs = w.reshape(SGU_GROUPS // 2, 2, SGU_CHUNK, SGU_CHUNK).transpose(0, 2, 1, 3)
    pairs = pairs.reshape(SGU_GROUPS // 2, SGU_CHUNK, 2 * SGU_CHUNK)
    bs_full = jnp.repeat(b_s.T.astype(_F32), SGU_GROUP_DIM, axis=1)
    return pairs, bs_full


def kernel(x, norm_g, w_in, sgu_ln_g, sgu_ln_b, sgu_w, sgu_b, w_out, rel_bias, final_g):
    batch, seq, _ = x.shape
    depth = norm_g.shape[0]
    nq = seq // MOBA_BLOCK
    assert seq % MOBA_BLOCK == 0 and nq % FAR_GROUP == 0 and nq <= DEAD_LANE
    x2d = x.reshape(batch * seq, D_MODEL)
    bias_rows, far_rows = _bias_tables(rel_bias)
    ohk = (jnp.arange(seq, dtype=jnp.int32)[:, None] // MOBA_BLOCK
           == jnp.arange(LANES, dtype=jnp.int32)[None, :]).astype(_BF16)
    fg = final_g.reshape(1, D_MODEL)
    for l in range(depth):
        ws_pairs, bs_full = _sgu_tables(sgu_w[l], sgu_b[l])
        w_bf = w_in[l].astype(_BF16)
        wvt_bf = w_bf[:, V_SEG * D_SGU:(V_SEG + 1) * D_SGU].T
        ya, q, k, vt, zb, ksum = _in_proj(
            x2d, norm_g[l].reshape(1, D_MODEL), w_bf, wvt_bf,
            sgu_ln_g[l].reshape(1, D_SGU), sgu_ln_b[l].reshape(1, D_SGU), ws_pairs, bs_full)
        kmean = ksum.reshape(batch, nq, HEAD_PAIRS, LANES) * (1.0 / MOBA_BLOCK)
        kmt = jnp.pad(kmean.transpose(0, 2, 3, 1), ((0, 0), (0, 0), (0, 0), (0, LANES - nq)))
        yb = _moba(q, k, ohk, vt, kmt, zb, bias_rows, far_rows, batch, seq)
        x2d = _out_proj(ya, yb, x2d, w_out[l].astype(_BF16), fg, final=(l == depth - 1))
    return x2d.reshape(batch, seq, D_MODEL)
```

```python
import functools
import math

import jax
import jax.numpy as jnp
from jax import lax
from jax.experimental import pallas as pl
from jax.experimental.pallas import tpu as pltpu

D_MODEL = 1024
D_SGU = 512
D_ATT = 512
SGU_GROUPS = 8
SGU_GROUP_DIM = D_SGU // SGU_GROUPS
SGU_CHUNK = 128
ATT_HEADS = 8
HEAD_DIM = D_ATT // ATT_HEADS
MOBA_BLOCK = 256
MOBA_TOPK = 3
REL_BUCKETS = 32
REL_MAX_DIST = 128
NORM_EPS = 1e-6
LN_EPS = 1e-5
NEG = -1e30
LOG2E = math.log2(math.e)
D_IN = 3 * D_SGU + 4 * D_ATT
Q_SEG, V_SEG = 3, 5

LANES = 128
BF16_ROWS = 16
HEADS_PER_STEP = LANES // HEAD_DIM
HEAD_PAIRS = ATT_HEADS // HEADS_PER_STEP
STACK = HEADS_PER_STEP * MOBA_BLOCK
VT_ROWS = HEAD_DIM + BF16_ROWS
HALF = 2
HALF_ROWS = HALF * MOBA_BLOCK
FAR_GROUP_LOG2 = 2
FAR_GROUP = 1 << FAR_GROUP_LOG2
ROWS_IN = 256
ROWS_OUT = 512
VMEM_LIMIT = 48 * 1024 * 1024

_BF16 = jnp.bfloat16
_F32 = jnp.float32


def _dot(a, b):
    return jnp.dot(a, b, preferred_element_type=_F32)


def _dot_nt(a, b):
    return lax.dot_general(a, b, (((1,), (1,)), ((), ())), preferred_element_type=_F32)


def _in_proj_kernel(x_ref, g_ref, w_ref, wqvt_ref, lng_ref, lnb_ref, ws_ref, bs_ref,
                    ya_ref, qt_ref, k_ref, vt_ref, zb_ref, ksum_ref):
    x = x_ref[...]
    ms = jnp.mean(x * x, axis=-1, keepdims=True)
    h = (x * lax.rsqrt(ms + NORM_EPS) * g_ref[...]).astype(_BF16)

    def proj(seg):
        return _dot(h, w_ref[:, seg * D_SGU:(seg + 1) * D_SGU])

    gv = jax.nn.gelu(proj(1))
    mu = jnp.mean(gv, axis=-1, keepdims=True)
    var = jnp.mean(jnp.square(gv - mu), axis=-1, keepdims=True)
    vln = (gv - mu) * lax.rsqrt(var + LN_EPS) * lng_ref[...] + lnb_ref[...]

    gate = jax.nn.gelu(proj(0)) * jax.nn.silu(proj(2))

    lane = lax.broadcasted_iota(jnp.int32, (SGU_CHUNK, LANES), 1)
    low = lane < SGU_GROUP_DIM
    for c in range(ROWS_IN // SGU_CHUNK):
        rows = slice(c * SGU_CHUNK, (c + 1) * SGU_CHUNK)
        for p in range(D_SGU // LANES):
            cols = slice(p * LANES, (p + 1) * LANES)
            vp = vln[rows, cols]
            rhs = jnp.concatenate([jnp.where(low, vp, 0.0).astype(_BF16),
                                   jnp.where(low, 0.0, vp).astype(_BF16)], axis=0)
            mixed = _dot(ws_ref[p], rhs) + bs_ref[:, cols]
            ya_ref[rows, cols] = (gate[rows, cols] * mixed).astype(_BF16)

    k = proj(4)
    k_ref[...] = k.astype(_BF16)
    ksum_ref[...] = jnp.sum(k, axis=0, keepdims=True)[None]
    zb_ref[...] = proj(6)

    qvt = _dot_nt(wqvt_ref[...], h)
    qt_ref[0] = (qvt[0:D_ATT] * (HEAD_DIM ** -0.5 * LOG2E)).astype(_BF16)
    vt = qvt[D_ATT:2 * D_ATT].astype(_BF16)
    ones = jnp.ones((BF16_ROWS, ROWS_IN), _BF16)
    for hd in range(ATT_HEADS):
        vt_ref[0, hd * VT_ROWS:hd * VT_ROWS + HEAD_DIM, :] = vt[hd * HEAD_DIM:(hd + 1) * HEAD_DIM, :]
        vt_ref[0, hd * VT_ROWS + HEAD_DIM:(hd + 1) * VT_ROWS, :] = ones


def _in_proj(x2d, g, w_bf, wqvt_bf, lng, lnb, ws_pairs, bs_full):
    n = x2d.shape[0]
    nblk = n // ROWS_IN
    row = lambda i: (i, 0)
    const2 = lambda i: (0, 0)
    seg_spec = pl.BlockSpec((ROWS_IN, D_ATT), row)
    return pl.pallas_call(
        _in_proj_kernel,
        grid=(nblk,),
        in_specs=[
            pl.BlockSpec((ROWS_IN, D_MODEL), row),
            pl.BlockSpec((1, D_MODEL), const2),
            pl.BlockSpec((D_MODEL, D_IN), const2),
            pl.BlockSpec((2 * D_ATT, D_MODEL), const2),
            pl.BlockSpec((1, D_SGU), const2),
            pl.BlockSpec((1, D_SGU), const2),
            pl.BlockSpec((D_SGU // LANES, SGU_CHUNK, 2 * SGU_CHUNK), lambda i: (0, 0, 0)),
            pl.BlockSpec((SGU_CHUNK, D_SGU), const2),
        ],
        out_specs=[
            seg_spec,
            pl.BlockSpec((1, D_ATT, ROWS_IN), lambda i: (i, 0, 0)),
            seg_spec,
            pl.BlockSpec((1, ATT_HEADS * VT_ROWS, ROWS_IN), lambda i: (i, 0, 0)),
            seg_spec,
            pl.BlockSpec((1, 1, D_ATT), lambda i: (i, 0, 0)),
        ],
        out_shape=[
            jax.ShapeDtypeStruct((n, D_SGU), _BF16),
            jax.ShapeDtypeStruct((nblk, D_ATT, ROWS_IN), _BF16),
            jax.ShapeDtypeStruct((n, D_ATT), _BF16),
            jax.ShapeDtypeStruct((nblk, ATT_HEADS * VT_ROWS, ROWS_IN), _BF16),
            jax.ShapeDtypeStruct((n, D_ATT), _F32),
            jax.ShapeDtypeStruct((nblk, 1, D_ATT), _F32),
        ],
        compiler_params=pltpu.CompilerParams(
            dimension_semantics=("arbitrary",), vmem_limit_bytes=VMEM_LIMIT),
        name="in_proj",
    )(x2d, g, w_bf, wqvt_bf, lng, lnb, ws_pairs, bs_full)


def _toeplitz(row_ref, hh, t):
    wide = jnp.broadcast_to(row_ref[hh, t], (MOBA_BLOCK, 2 * MOBA_BLOCK))
    return pltpu.roll(wide, 0, 1, stride=1, stride_axis=0)[:, 0:MOBA_BLOCK]


def _moba_kernel(qt_ref, k_ref, ohk_ref, vt_ref, ksum_ref, zb_ref, brow_ref, far_ref,
                 yb_ref, qaug_ref, qfar_ref, bias_ref, m_ref, acc_ref, sa_ref, sb_ref):
    i = pl.program_id(2)
    blk = MOBA_BLOCK
    nq = ksum_ref.shape[0]

    @pl.when(i == 0)
    def _():
        for hh in range(HEADS_PER_STEP):
            for t in range(2):
                bias_ref[t * blk:(t + 1) * blk, hh * blk:(hh + 1) * blk] = _toeplitz(brow_ref, hh, t)

    qf = qt_ref[...].astype(_F32)
    ch = lax.broadcasted_iota(jnp.int32, qf.shape, 0)
    qh = jnp.concatenate([jnp.where(ch < HEAD_DIM, qf, 0.0),
                          jnp.where(ch < HEAD_DIM, 0.0, qf)], axis=1)
    kmean = ksum_ref[...] * (1.0 / blk)
    g = jnp.dot(kmean, qh, preferred_element_type=_F32, precision=lax.Precision.HIGHEST)
    row = lax.broadcasted_iota(jnp.int32, (nq, STACK), 0)
    row_f = row.astype(_F32)
    g = jnp.where(row < i, g, NEG)
    sel = jnp.zeros((nq, STACK), jnp.bool_)
    for r in range(MOBA_TOPK):
        top = jnp.max(g, axis=0, keepdims=True)
        idx = jnp.min(jnp.where(g == top, row_f, float(nq)), axis=0, keepdims=True)
        pick = row_f == idx
        sel = sel | (pick & (jnp.full_like(row, r) < i))
        g = jnp.where(pick, -3e38, g)
    selb = jnp.where(sel, far_ref[...], NEG)
    qh16 = qh.astype(_BF16)
    pad = jnp.zeros((LANES - nq, STACK), _BF16)
    qaug_ref[0:LANES] = qh16
    qaug_ref[LANES:LANES + nq] = selb.astype(_BF16)
    qaug_ref[LANES + nq:2 * LANES] = pad
    qfar_ref[0:LANES] = qh16
    qfar_ref[LANES:LANES + nq] = jnp.where(row == i - 1, NEG, selb).astype(_BF16)
    qfar_ref[LANES + nq:2 * LANES] = pad

    m_ref[...] = jnp.full(m_ref.shape, NEG, _F32)
    acc_ref[...] = jnp.zeros(acc_ref.shape, _F32)

    def key_rows(j0, nblk):
        return pl.ds(pl.multiple_of(j0 * blk, blk), nblk * blk)

    def online_update(s_ref, blocks):
        s = s_ref[...]
        m_prev = m_ref[...]
        m_new = jnp.maximum(m_prev, jnp.max(s, axis=0, keepdims=True))
        alpha = jnp.exp2(m_prev - m_new)
        p = jnp.exp2(s - m_new).astype(_BF16)
        for hh in range(HEADS_PER_STEP):
            cols = slice(hh * blk, (hh + 1) * blk)
            vrows = slice(hh * VT_ROWS, (hh + 1) * VT_ROWS)
            o = _dot(vt_ref[blocks[0], vrows, :], p[0:blk, cols])
            for n_, b_ in enumerate(blocks[1:], start=1):
                o = o + _dot(vt_ref[b_, vrows, :], p[n_ * blk:(n_ + 1) * blk, cols])
            acc_ref[hh] = alpha[:, cols] * acc_ref[hh] + o
        m_ref[...] = m_new

    def near_scores():
        jp = jnp.maximum(i - 1, 0)
        own = jnp.concatenate([k_ref[key_rows(i, 1), :], jnp.zeros((blk, LANES), _BF16)], axis=1)
        prev = jnp.concatenate([k_ref[key_rows(jp, 1), :], ohk_ref[key_rows(jp, 1), :]], axis=1)
        return _dot(jnp.concatenate([own, prev], axis=0), qaug_ref[...]) + bias_ref[...]

    def far_scores(hf):
        rows = key_rows(hf * HALF, HALF)
        return _dot(jnp.concatenate([k_ref[rows, :], ohk_ref[rows, :]], axis=1), qfar_ref[...])

    def far_blocks(hf):
        return (hf * HALF, hf * HALF + 1)

    near = (i, jnp.maximum(i - 1, 0))
    n_far = jnp.maximum(i - 1, 0)
    n_steps = lax.shift_right_logical(n_far + (FAR_GROUP - 1), FAR_GROUP_LOG2)

    sa_ref[...] = near_scores()

    @pl.when(n_steps == 0)
    def _():
        online_update(sa_ref, near)

    @pl.when(n_steps >= 1)
    def _():
        sb_ref[...] = far_scores(0)
        online_update(sa_ref, near)

        def far_step(t, carry):
            hf = 2 * t
            sa_ref[...] = far_scores(hf + 1)
            online_update(sb_ref, far_blocks(hf))
            sb_ref[...] = far_scores(hf + 2)
            online_update(sa_ref, far_blocks(hf + 1))
            return carry

        lax.fori_loop(0, n_steps - 1, far_step, 0)
        hl = 2 * (n_steps - 1)
        sa_ref[...] = far_scores(hl + 1)
        online_update(sb_ref, far_blocks(hl))
        online_update(sa_ref, far_blocks(hl + 1))

    o_t = jnp.concatenate(
        [acc_ref[hh, 0:HEAD_DIM, :] / acc_ref[hh, HEAD_DIM:HEAD_DIM + 1, :]
         for hh in range(HEADS_PER_STEP)], axis=0)
    yb_ref[...] = (o_t.T * jax.nn.silu(zb_ref[...])).astype(_BF16)


def _moba(qt, k, ohk, vt, ksum, zb, bias_rows, far_rows, batch, seq):
    n = k.shape[0]
    nq = seq // MOBA_BLOCK
    tile = pl.BlockSpec((MOBA_BLOCK, LANES), lambda b, hp, i: (b * nq + i, hp))
    return pl.pallas_call(
        _moba_kernel,
        grid=(batch, HEAD_PAIRS, nq),
        in_specs=[
            pl.BlockSpec((None, LANES, MOBA_BLOCK), lambda b, hp, i: (b * nq + i, hp, 0)),
            pl.BlockSpec((seq, LANES), lambda b, hp, i: (b, hp)),
            pl.BlockSpec((seq, LANES), lambda b, hp, i: (0, 0)),
            pl.BlockSpec((nq, HEADS_PER_STEP * VT_ROWS, MOBA_BLOCK), lambda b, hp, i: (b, hp, 0)),
            pl.BlockSpec((nq, LANES), lambda b, hp, i: (b, hp)),
            tile,
            pl.BlockSpec((HEADS_PER_STEP, 2, 1, 2 * MOBA_BLOCK), lambda b, hp, i: (hp, 0, 0, 0)),
            pl.BlockSpec((None, 1, STACK), lambda b, hp, i: (hp, 0, 0)),
        ],
        out_specs=tile,
        out_shape=jax.ShapeDtypeStruct((n, D_ATT), _BF16),
        scratch_shapes=[
            pltpu.VMEM((2 * LANES, STACK), _BF16),
            pltpu.VMEM((2 * LANES, STACK), _BF16),
            pltpu.VMEM((HALF_ROWS, STACK), _F32),
            pltpu.VMEM((1, STACK), _F32),
            pltpu.VMEM((HEADS_PER_STEP, VT_ROWS, MOBA_BLOCK), _F32),
            pltpu.VMEM((HALF_ROWS, STACK), _F32),
            pltpu.VMEM((HALF_ROWS, STACK), _F32),
        ],
        compiler_params=pltpu.CompilerParams(
            dimension_semantics=("arbitrary", "arbitrary", "arbitrary"),
            vmem_limit_bytes=VMEM_LIMIT),
        name="moba",
    )(qt, k, ohk, vt, ksum, zb, bias_rows, far_rows)


def _out_proj_kernel(ya_ref, yb_ref, x_ref, w_ref, g_ref, o_ref, *, final):
    y = (x_ref[...] + _dot(ya_ref[...], w_ref[0:D_SGU, :])
         + _dot(yb_ref[...], w_ref[D_SGU:D_SGU + D_ATT, :]))
    if final:
        ms = jnp.mean(y * y, axis=-1, keepdims=True)
        y = y * lax.rsqrt(ms + NORM_EPS) * g_ref[...]
    o_ref[...] = y


def _out_proj(ya, yb, x2d, w_bf, final_g, final):
    n = x2d.shape[0]
    row = lambda i: (i, 0)
    const2 = lambda i: (0, 0)
    return pl.pallas_call(
        functools.partial(_out_proj_kernel, final=final),
        grid=(n // ROWS_OUT,),
        in_specs=[
            pl.BlockSpec((ROWS_OUT, D_SGU), row),
            pl.BlockSpec((ROWS_OUT, D_ATT), row),
            pl.BlockSpec((ROWS_OUT, D_MODEL), row),
            pl.BlockSpec((D_SGU + D_ATT, D_MODEL), const2),
            pl.BlockSpec((1, D_MODEL), const2),
        ],
        out_specs=pl.BlockSpec((ROWS_OUT, D_MODEL), row),
        out_shape=jax.ShapeDtypeStruct((n, D_MODEL), _F32),
        compiler_params=pltpu.CompilerParams(
            dimension_semantics=("arbitrary",), vmem_limit_bytes=VMEM_LIMIT),
        name="out_proj",
    )(ya, yb, x2d, w_bf, final_g)


def _rel_bucket(dist):
    n = jnp.maximum(dist, 0)
    max_exact = REL_BUCKETS // 2
    nf = jnp.maximum(n, 1).astype(_F32)
    large = max_exact + (jnp.log(nf / max_exact) / math.log(REL_MAX_DIST / max_exact)
                         * (REL_BUCKETS - max_exact)).astype(jnp.int32)
    large = jnp.minimum(large, REL_BUCKETS - 1)
    return jnp.where(n < max_exact, n, large)


def _bias_tables(rel_bias):
    table = rel_bias.T.astype(_F32) * LOG2E
    period = 2 * MOBA_BLOCK
    u = jnp.arange(period, dtype=jnp.int32)
    far = table[:, REL_BUCKETS - 1]
    own = jnp.where((u < MOBA_BLOCK)[None], table[:, _rel_bucket(u)], NEG)
    prev = table[:, _rel_bucket((u + MOBA_BLOCK) % period)] - far[:, None]
    rows = jnp.stack([own, prev], axis=1)[:, :, None, :]
    far_rows = jnp.repeat(far, MOBA_BLOCK).reshape(HEAD_PAIRS, 1, STACK)
    return rows, far_rows


def _sgu_tables(w_s, b_s):
    causal = jnp.tril(jnp.ones((SGU_CHUNK, SGU_CHUNK), dtype=bool))
    w = jnp.where(causal[None], w_s, 0.0).astype(_BF16)
    pairs = w.reshape(SGU_GROUPS // 2, 2, SGU_CHUNK, SGU_CHUNK).transpose(0, 2, 1, 3)
    pairs = pairs.reshape(SGU_GROUPS // 2, SGU_CHUNK, 2 * SGU_CHUNK)
    bs_full = jnp.repeat(b_s.T.astype(_F32), SGU_GROUP_DIM, axis=1)
    return pairs, bs_full


def kernel(x, norm_g, w_in, sgu_ln_g, sgu_ln_b, sgu_w, sgu_b, w_out, rel_bias, final_g):
    batch, seq, _ = x.shape
    depth = norm_g.shape[0]
    nq = seq // MOBA_BLOCK
    assert seq % MOBA_BLOCK == 0 and nq % FAR_GROUP == 0 and nq % BF16_ROWS == 0 and nq <= LANES
    x2d = x.reshape(batch * seq, D_MODEL)
    bias_rows, far_rows = _bias_tables(rel_bias)
    ohk = (jnp.arange(seq, dtype=jnp.int32)[:, None] // MOBA_BLOCK
           == jnp.arange(LANES, dtype=jnp.int32)[None, :]).astype(_BF16)
    fg = final_g.reshape(1, D_MODEL)
    for l in range(depth):
        ws_pairs, bs_full = _sgu_tables(sgu_w[l], sgu_b[l])
        w_bf = w_in[l].astype(_BF16)
        wqvt_bf = jnp.concatenate([w_bf[:, Q_SEG * D_SGU:(Q_SEG + 1) * D_SGU],
                                   w_bf[:, V_SEG * D_SGU:(V_SEG + 1) * D_SGU]], axis=1).T
        ya, qt, k, vt, zb, ksum = _in_proj(
            x2d, norm_g[l].reshape(1, D_MODEL), w_bf, wqvt_bf,
            sgu_ln_g[l].reshape(1, D_SGU), sgu_ln_b[l].reshape(1, D_SGU), ws_pairs, bs_full)
        yb = _moba(qt, k, ohk, vt, ksum.reshape(batch * nq, D_ATT), zb,
                   bias_rows, far_rows, batch, seq)
        x2d = _out_proj(ya, yb, x2d, w_out[l].astype(_BF16), fg, final=(l == depth - 1))
    return x2d.reshape(batch, seq, D_MODEL)
```

```python
import functools
import math

import jax
import jax.numpy as jnp
from jax import lax
from jax.experimental import pallas as pl
from jax.experimental.pallas import tpu as pltpu

D_MODEL = 1024
D_SGU = 512
D_ATT = 512
SGU_GROUPS = 8
SGU_GROUP_DIM = D_SGU // SGU_GROUPS
SGU_CHUNK = 128
ATT_HEADS = 8
HEAD_DIM = D_ATT // ATT_HEADS
MOBA_BLOCK = 256
MOBA_TOPK = 3
REL_BUCKETS = 32
REL_MAX_DIST = 128
NORM_EPS = 1e-6
LN_EPS = 1e-5
NEG = -1e30
LOG2E = math.log2(math.e)
D_IN = 3 * D_SGU + 4 * D_ATT
Q_SEG, V_SEG = 3, 5

LANES = 128
BF16_ROWS = 16
HEADS_PER_STEP = LANES // HEAD_DIM
HEAD_PAIRS = ATT_HEADS // HEADS_PER_STEP
STACK = HEADS_PER_STEP * MOBA_BLOCK
VT_ROWS = HEAD_DIM + BF16_ROWS
HALF_LOG2 = 2
HALF = 1 << HALF_LOG2
HALF_ROWS = HALF * MOBA_BLOCK
NEAR_ROWS = 2 * MOBA_BLOCK
ROWS_IN = 256
ROWS_OUT = 512
VMEM_LIMIT = 48 * 1024 * 1024

_BF16 = jnp.bfloat16
_F32 = jnp.float32


def _dot(a, b):
    return jnp.dot(a, b, preferred_element_type=_F32)


def _dot_nt(a, b):
    return lax.dot_general(a, b, (((1,), (1,)), ((), ())), preferred_element_type=_F32)


def _in_proj_kernel(x_ref, g_ref, w_ref, wqvt_ref, lng_ref, lnb_ref, ws_ref, bs_ref,
                    ya_ref, qt_ref, k_ref, vt_ref, zb_ref, ksum_ref):
    x = x_ref[...]
    ms = jnp.mean(x * x, axis=-1, keepdims=True)
    h = (x * lax.rsqrt(ms + NORM_EPS) * g_ref[...]).astype(_BF16)

    def proj(seg):
        return _dot(h, w_ref[:, seg * D_SGU:(seg + 1) * D_SGU])

    gv = jax.nn.gelu(proj(1))
    mu = jnp.mean(gv, axis=-1, keepdims=True)
    var = jnp.mean(jnp.square(gv - mu), axis=-1, keepdims=True)
    vln = (gv - mu) * lax.rsqrt(var + LN_EPS) * lng_ref[...] + lnb_ref[...]

    gate = jax.nn.gelu(proj(0)) * jax.nn.silu(proj(2))

    lane = lax.broadcasted_iota(jnp.int32, (SGU_CHUNK, LANES), 1)
    low = lane < SGU_GROUP_DIM
    for c in range(ROWS_IN // SGU_CHUNK):
        rows = slice(c * SGU_CHUNK, (c + 1) * SGU_CHUNK)
        for p in range(D_SGU // LANES):
            cols = slice(p * LANES, (p + 1) * LANES)
            vp = vln[rows, cols]
            rhs = jnp.concatenate([jnp.where(low, vp, 0.0).astype(_BF16),
                                   jnp.where(low, 0.0, vp).astype(_BF16)], axis=0)
            mixed = _dot(ws_ref[p], rhs) + bs_ref[:, cols]
            ya_ref[rows, cols] = (gate[rows, cols] * mixed).astype(_BF16)

    k = proj(4)
    k_ref[...] = k.astype(_BF16)
    ksum_ref[...] = jnp.sum(k, axis=0, keepdims=True)[None]
    zb_ref[...] = proj(6)

    qvt = _dot_nt(wqvt_ref[...], h)
    qt_ref[0] = (qvt[0:D_ATT] * (HEAD_DIM ** -0.5 * LOG2E)).astype(_BF16)
    vt = qvt[D_ATT:2 * D_ATT].astype(_BF16)
    ones = jnp.ones((BF16_ROWS, ROWS_IN), _BF16)
    for hd in range(ATT_HEADS):
        vt_ref[0, hd * VT_ROWS:hd * VT_ROWS + HEAD_DIM, :] = vt[hd * HEAD_DIM:(hd + 1) * HEAD_DIM, :]
        vt_ref[0, hd * VT_ROWS + HEAD_DIM:(hd + 1) * VT_ROWS, :] = ones


def _in_proj(x2d, g, w_bf, wqvt_bf, lng, lnb, ws_pairs, bs_full):
    n = x2d.shape[0]
    nblk = n // ROWS_IN
    row = lambda i: (i, 0)
    const2 = lambda i: (0, 0)
    seg_spec = pl.BlockSpec((ROWS_IN, D_ATT), row)
    return pl.pallas_call(
        _in_proj_kernel,
        grid=(nblk,),
        in_specs=[
            pl.BlockSpec((ROWS_IN, D_MODEL), row),
            pl.BlockSpec((1, D_MODEL), const2),
            pl.BlockSpec((D_MODEL, D_IN), const2),
            pl.BlockSpec((2 * D_ATT, D_MODEL), const2),
            pl.BlockSpec((1, D_SGU), const2),
            pl.BlockSpec((1, D_SGU), const2),
            pl.BlockSpec((D_SGU // LANES, SGU_CHUNK, 2 * SGU_CHUNK), lambda i: (0, 0, 0)),
            pl.BlockSpec((SGU_CHUNK, D_SGU), const2),
        ],
        out_specs=[
            seg_spec,
            pl.BlockSpec((1, D_ATT, ROWS_IN), lambda i: (i, 0, 0)),
            seg_spec,
            pl.BlockSpec((1, ATT_HEADS * VT_ROWS, ROWS_IN), lambda i: (i, 0, 0)),
            seg_spec,
            pl.BlockSpec((1, 1, D_ATT), lambda i: (i, 0, 0)),
        ],
        out_shape=[
            jax.ShapeDtypeStruct((n, D_SGU), _BF16),
            jax.ShapeDtypeStruct((nblk, D_ATT, ROWS_IN), _BF16),
            jax.ShapeDtypeStruct((n, D_ATT), _BF16),
            jax.ShapeDtypeStruct((nblk, ATT_HEADS * VT_ROWS, ROWS_IN), _BF16),
            jax.ShapeDtypeStruct((n, D_ATT), _F32),
            jax.ShapeDtypeStruct((nblk, 1, D_ATT), _F32),
        ],
        compiler_params=pltpu.CompilerParams(
            dimension_semantics=("arbitrary",), vmem_limit_bytes=VMEM_LIMIT),
        name="in_proj",
    )(x2d, g, w_bf, wqvt_bf, lng, lnb, ws_pairs, bs_full)


def _toeplitz(row_ref, hh, t):
    wide = jnp.broadcast_to(row_ref[hh, t], (MOBA_BLOCK, 2 * MOBA_BLOCK))
    return pltpu.roll(wide, 0, 1, stride=1, stride_axis=0)[:, 0:MOBA_BLOCK]


def _moba_kernel(qt_ref, k_ref, ohk_ref, vt_ref, ksum_ref, zb_ref, brow_ref, far_ref,
                 yb_ref, qaug_ref, qfar_ref, bias_ref, m_ref, acc_ref,
                 sa_ref, sb_ref, ma_ref, mb_ref):
    i = pl.program_id(2)
    blk = MOBA_BLOCK
    nq = ksum_ref.shape[0]

    @pl.when(i == 0)
    def _():
        for hh in range(HEADS_PER_STEP):
            for t in range(2):
                bias_ref[t * blk:(t + 1) * blk, hh * blk:(hh + 1) * blk] = _toeplitz(brow_ref, hh, t)

    qf = qt_ref[...].astype(_F32)
    ch = lax.broadcasted_iota(jnp.int32, qf.shape, 0)
    qh = jnp.concatenate([jnp.where(ch < HEAD_DIM, qf, 0.0),
                          jnp.where(ch < HEAD_DIM, 0.0, qf)], axis=1)
    kmean = ksum_ref[...] * (1.0 / blk)
    g = jnp.dot(kmean, qh, preferred_element_type=_F32, precision=lax.Precision.HIGHEST)
    row = lax.broadcasted_iota(jnp.int32, (nq, STACK), 0)
    row_f = row.astype(_F32)
    g = jnp.where(row < i, g, NEG)
    sel = jnp.zeros((nq, STACK), jnp.bool_)
    for r in range(MOBA_TOPK):
        top = jnp.max(g, axis=0, keepdims=True)
        idx = jnp.min(jnp.where(g == top, row_f, float(nq)), axis=0, keepdims=True)
        pick = row_f == idx
        sel = sel | (pick & (jnp.full_like(row, r) < i))
        g = jnp.where(pick, -3e38, g)
    selb = jnp.where(sel, far_ref[...], NEG)
    qh16 = qh.astype(_BF16)
    pad = jnp.zeros((LANES - nq, STACK), _BF16)
    qaug_ref[0:LANES] = qh16
    qaug_ref[LANES:LANES + nq] = selb.astype(_BF16)
    qaug_ref[LANES + nq:2 * LANES] = pad
    qfar_ref[0:LANES] = qh16
    qfar_ref[LANES:LANES + nq] = jnp.where(row == i - 1, NEG, selb).astype(_BF16)
    qfar_ref[LANES + nq:2 * LANES] = pad

    m_ref[...] = jnp.full(m_ref.shape, NEG, _F32)
    acc_ref[...] = jnp.zeros(acc_ref.shape, _F32)

    def key_rows(j0, nblk):
        return pl.ds(pl.multiple_of(j0 * blk, blk), nblk * blk)

    def produce(buf, s):
        s_ref, smax_ref = buf
        s_ref[0:s.shape[0]] = s
        smax_ref[...] = jnp.max(s, axis=0, keepdims=True)

    def online_update(buf, blocks):
        s_ref, smax_ref = buf
        s = s_ref[0:len(blocks) * blk]
        m_prev = m_ref[...]
        m_new = jnp.maximum(m_prev, smax_ref[...])
        alpha = jnp.exp2(m_prev - m_new)
        p = jnp.exp2(s - m_new).astype(_BF16)
        for hh in range(HEADS_PER_STEP):
            cols = slice(hh * blk, (hh + 1) * blk)
            vrows = slice(hh * VT_ROWS, (hh + 1) * VT_ROWS)
            o = _dot(vt_ref[blocks[0], vrows, :], p[0:blk, cols])
            for n_, b_ in enumerate(blocks[1:], start=1):
                o = o + _dot(vt_ref[b_, vrows, :], p[n_ * blk:(n_ + 1) * blk, cols])
            acc_ref[hh] = alpha[:, cols] * acc_ref[hh] + o
        m_ref[...] = m_new

    def near_scores():
        jp = jnp.maximum(i - 1, 0)
        own = jnp.concatenate([k_ref[key_rows(i, 1), :], jnp.zeros((blk, LANES), _BF16)], axis=1)
        prev = jnp.concatenate([k_ref[key_rows(jp, 1), :], ohk_ref[key_rows(jp, 1), :]], axis=1)
        return _dot(jnp.concatenate([own, prev], axis=0), qaug_ref[...]) + bias_ref[...]

    def far_scores(hf):
        rows = key_rows(hf * HALF, HALF)
        return _dot(jnp.concatenate([k_ref[rows, :], ohk_ref[rows, :]], axis=1), qfar_ref[...])

    def far_blocks(hf):
        return tuple(hf * HALF + b_ for b_ in range(HALF))

    near = (i, jnp.maximum(i - 1, 0))
    n_far = jnp.maximum(i - 1, 0)
    n_half = lax.shift_right_logical(n_far + (HALF - 1), HALF_LOG2)
    buf_a = (sa_ref, ma_ref)
    buf_b = (sb_ref, mb_ref)

    produce(buf_a, near_scores())

    @pl.when(n_half == 0)
    def _():
        online_update(buf_a, near)

    @pl.when(n_half >= 1)
    def _():
        produce(buf_b, far_scores(0))
        online_update(buf_a, near)
        n_pairs = lax.shift_right_logical(n_half - 1, 1)

        def far_step(t, carry):
            hf = 2 * t
            produce(buf_a, far_scores(hf + 1))
            online_update(buf_b, far_blocks(hf))
            produce(buf_b, far_scores(hf + 2))
            online_update(buf_a, far_blocks(hf + 1))
            return carry

        lax.fori_loop(0, n_pairs, far_step, 0)
        hl = 2 * n_pairs

        @pl.when(n_half - hl == 2)
        def _():
            produce(buf_a, far_scores(hl + 1))
            online_update(buf_b, far_blocks(hl))
            online_update(buf_a, far_blocks(hl + 1))

        @pl.when(n_half - hl == 1)
        def _():
            online_update(buf_b, far_blocks(hl))

    o_t = jnp.concatenate(
        [acc_ref[hh, 0:HEAD_DIM, :] / acc_ref[hh, HEAD_DIM:HEAD_DIM + 1, :]
         for hh in range(HEADS_PER_STEP)], axis=0)
    yb_ref[...] = (o_t.T * jax.nn.silu(zb_ref[...])).astype(_BF16)


def _moba(qt, k, ohk, vt, ksum, zb, bias_rows, far_rows, batch, seq):
    n = k.shape[0]
    nq = seq // MOBA_BLOCK
    tile = pl.BlockSpec((MOBA_BLOCK, LANES), lambda b, hp, i: (b * nq + i, hp))
    return pl.pallas_call(
        _moba_kernel,
        grid=(batch, HEAD_PAIRS, nq),
        in_specs=[
            pl.BlockSpec((None, LANES, MOBA_BLOCK), lambda b, hp, i: (b * nq + i, hp, 0)),
            pl.BlockSpec((seq, LANES), lambda b, hp, i: (b, hp)),
            pl.BlockSpec((seq, LANES), lambda b, hp, i: (0, 0)),
            pl.BlockSpec((nq, HEADS_PER_STEP * VT_ROWS, MOBA_BLOCK), lambda b, hp, i: (b, hp, 0)),
            pl.BlockSpec((nq, LANES), lambda b, hp, i: (b, hp)),
            tile,
            pl.BlockSpec((HEADS_PER_STEP, 2, 1, 2 * MOBA_BLOCK), lambda b, hp, i: (hp, 0, 0, 0)),
            pl.BlockSpec((None, 1, STACK), lambda b, hp, i: (hp, 0, 0)),
        ],
        out_specs=tile,
        out_shape=jax.ShapeDtypeStruct((n, D_ATT), _BF16),
        scratch_shapes=[
            pltpu.VMEM((2 * LANES, STACK), _BF16),
            pltpu.VMEM((2 * LANES, STACK), _BF16),
            pltpu.VMEM((NEAR_ROWS, STACK), _F32),
            pltpu.VMEM((1, STACK), _F32),
            pltpu.VMEM((HEADS_PER_STEP, VT_ROWS, MOBA_BLOCK), _F32),
            pltpu.VMEM((HALF_ROWS, STACK), _F32),
            pltpu.VMEM((HALF_ROWS, STACK), _F32),
            pltpu.VMEM((1, STACK), _F32),
            pltpu.VMEM((1, STACK), _F32),
        ],
        compiler_params=pltpu.CompilerParams(
            dimension_semantics=("arbitrary", "arbitrary", "arbitrary"),
            vmem_limit_bytes=VMEM_LIMIT),
        name="moba",
    )(qt, k, ohk, vt, ksum, zb, bias_rows, far_rows)


def _out_proj_kernel(ya_ref, yb_ref, x_ref, w_ref, g_ref, o_ref, *, final):
    y = (x_ref[...] + _dot(ya_ref[...], w_ref[0:D_SGU, :])
         + _dot(yb_ref[...], w_ref[D_SGU:D_SGU + D_ATT, :]))
    if final:
        ms = jnp.mean(y * y, axis=-1, keepdims=True)
        y = y * lax.rsqrt(ms + NORM_EPS) * g_ref[...]
    o_ref[...] = y


def _out_proj(ya, yb, x2d, w_bf, final_g, final):
    n = x2d.shape[0]
    row = lambda i: (i, 0)
    const2 = lambda i: (0, 0)
    return pl.pallas_call(
        functools.partial(_out_proj_kernel, final=final),
        grid=(n // ROWS_OUT,),
        in_specs=[
            pl.BlockSpec((ROWS_OUT, D_SGU), row),
            pl.BlockSpec((ROWS_OUT, D_ATT), row),
            pl.BlockSpec((ROWS_OUT, D_MODEL), row),
            pl.BlockSpec((D_SGU + D_ATT, D_MODEL), const2),
            pl.BlockSpec((1, D_MODEL), const2),
        ],
        out_specs=pl.BlockSpec((ROWS_OUT, D_MODEL), row),
        out_shape=jax.ShapeDtypeStruct((n, D_MODEL), _F32),
        compiler_params=pltpu.CompilerParams(
            dimension_semantics=("arbitrary",), vmem_limit_bytes=VMEM_LIMIT),
        name="out_proj",
    )(ya, yb, x2d, w_bf, final_g)


def _rel_bucket(dist):
    n = jnp.maximum(dist, 0)
    max_exact = REL_BUCKETS // 2
    nf = jnp.maximum(n, 1).astype(_F32)
    large = max_exact + (jnp.log(nf / max_exact) / math.log(REL_MAX_DIST / max_exact)
                         * (REL_BUCKETS - max_exact)).astype(jnp.int32)
    large = jnp.minimum(large, REL_BUCKETS - 1)
    return jnp.where(n < max_exact, n, large)


def _bias_tables(rel_bias):
    table = rel_bias.T.astype(_F32) * LOG2E
    period = 2 * MOBA_BLOCK
    u = jnp.arange(period, dtype=jnp.int32)
    far = table[:, REL_BUCKETS - 1]
    own = jnp.where((u < MOBA_BLOCK)[None], table[:, _rel_bucket(u)], NEG)
    prev = table[:, _rel_bucket((u + MOBA_BLOCK) % period)] - far[:, None]
    rows = jnp.stack([own, prev], axis=1)[:, :, None, :]
    far_rows = jnp.repeat(far, MOBA_BLOCK).reshape(HEAD_PAIRS, 1, STACK)
    return rows, far_rows


def _sgu_tables(w_s, b_s):
    causal = jnp.tril(jnp.ones((SGU_CHUNK, SGU_CHUNK), dtype=bool))
    w = jnp.where(causal[None], w_s, 0.0).astype(_BF16)
    pairs = w.reshape(SGU_GROUPS // 2, 2, SGU_CHUNK, SGU_CHUNK).transpose(0, 2, 1, 3)
    pairs = pairs.reshape(SGU_GROUPS // 2, SGU_CHUNK, 2 * SGU_CHUNK)
    bs_full = jnp.repeat(b_s.T.astype(_F32), SGU_GROUP_DIM, axis=1)
    return pairs, bs_full


def kernel(x, norm_g, w_in, sgu_ln_g, sgu_ln_b, sgu_w, sgu_b, w_out, rel_bias, final_g):
    batch, seq, _ = x.shape
    depth = norm_g.shape[0]
    nq = seq // MOBA_BLOCK
    assert seq % MOBA_BLOCK == 0 and nq % HALF == 0 and nq % BF16_ROWS == 0 and nq <= LANES
    x2d = x.reshape(batch * seq, D_MODEL)
    bias_rows, far_rows = _bias_tables(rel_bias)
    ohk = (jnp.arange(seq, dtype=jnp.int32)[:, None] // MOBA_BLOCK
           == jnp.arange(LANES, dtype=jnp.int32)[None, :]).astype(_BF16)
    fg = final_g.reshape(1, D_MODEL)
    for l in range(depth):
        ws_pairs, bs_full = _sgu_tables(sgu_w[l], sgu_b[l])
        w_bf = w_in[l].astype(_BF16)
        wqvt_bf = jnp.concatenate([w_bf[:, Q_SEG * D_SGU:(Q_SEG + 1) * D_SGU],
                                   w_bf[:, V_SEG * D_SGU:(V_SEG + 1) * D_SGU]], axis=1).T
        ya, qt, k, vt, zb, ksum = _in_proj(
            x2d, norm_g[l].reshape(1, D_MODEL), w_bf, wqvt_bf,
            sgu_ln_g[l].reshape(1, D_SGU), sgu_ln_b[l].reshape(1, D_SGU), ws_pairs, bs_full)
        yb = _moba(qt, k, ohk, vt, ksum.reshape(batch * nq, D_ATT), zb,
                   bias_rows, far_rows, batch, seq)
        x2d = _out_proj(ya, yb, x2d, w_out[l].astype(_BF16), fg, final=(l == depth - 1))
    return x2d.reshape(batch, seq, D_MODEL)
```

```python
import functools
import math

import jax
import jax.numpy as jnp
from jax import lax
from jax.experimental import pallas as pl
from jax.experimental.pallas import tpu as pltpu

D_MODEL = 1024
D_SGU = 512
D_ATT = 512
SGU_GROUPS = 8
SGU_GROUP_DIM = D_SGU // SGU_GROUPS
SGU_CHUNK = 128
ATT_HEADS = 8
HEAD_DIM = D_ATT // ATT_HEADS
MOBA_BLOCK = 256
MOBA_TOPK = 3
REL_BUCKETS = 32
REL_MAX_DIST = 128
NORM_EPS = 1e-6
LN_EPS = 1e-5
NEG = -1e30
LOG2E = math.log2(math.e)
D_IN = 3 * D_SGU + 4 * D_ATT
Q_SEG, V_SEG = 3, 5

LANES = 128
BF16_ROWS = 16
HEADS_PER_STEP = LANES // HEAD_DIM
HEAD_PAIRS = ATT_HEADS // HEADS_PER_STEP
STACK = HEADS_PER_STEP * MOBA_BLOCK
VT_ROWS = HEAD_DIM + BF16_ROWS
HALF_LOG2 = 2
HALF = 1 << HALF_LOG2
HALF_ROWS = HALF * MOBA_BLOCK
NEAR_ROWS = 2 * MOBA_BLOCK
ROWS_IN = 256
ROWS_OUT = 512
VMEM_LIMIT = 48 * 1024 * 1024

_BF16 = jnp.bfloat16
_F32 = jnp.float32


def _dot(a, b):
    return jnp.dot(a, b, preferred_element_type=_F32)


def _dot_nt(a, b):
    return lax.dot_general(a, b, (((1,), (1,)), ((), ())), preferred_element_type=_F32)


def _in_proj_kernel(x_ref, g_ref, w_ref, wqvt_ref, lng_ref, lnb_ref, ws_ref, bs_ref,
                    ya_ref, qt_ref, k_ref, vt_ref, zb_ref, ksum_ref):
    x = x_ref[...]
    ms = jnp.mean(x * x, axis=-1, keepdims=True)
    h = (x * lax.rsqrt(ms + NORM_EPS) * g_ref[...]).astype(_BF16)

    def proj(seg):
        return _dot(h, w_ref[:, seg * D_SGU:(seg + 1) * D_SGU])

    gv = jax.nn.gelu(proj(1))
    mu = jnp.mean(gv, axis=-1, keepdims=True)
    var = jnp.mean(jnp.square(gv - mu), axis=-1, keepdims=True)
    vln = (gv - mu) * lax.rsqrt(var + LN_EPS) * lng_ref[...] + lnb_ref[...]

    gate = jax.nn.gelu(proj(0)) * jax.nn.silu(proj(2))

    lane = lax.broadcasted_iota(jnp.int32, (SGU_CHUNK, LANES), 1)
    low = lane < SGU_GROUP_DIM
    for c in range(ROWS_IN // SGU_CHUNK):
        rows = slice(c * SGU_CHUNK, (c + 1) * SGU_CHUNK)
        for p in range(D_SGU // LANES):
            cols = slice(p * LANES, (p + 1) * LANES)
            vp = vln[rows, cols]
            rhs = jnp.concatenate([jnp.where(low, vp, 0.0).astype(_BF16),
                                   jnp.where(low, 0.0, vp).astype(_BF16)], axis=0)
            mixed = _dot(ws_ref[p], rhs) + bs_ref[:, cols]
            ya_ref[rows, cols] = (gate[rows, cols] * mixed).astype(_BF16)

    k = proj(4)
    k_ref[...] = k.astype(_BF16)
    ksum_ref[...] = jnp.sum(k, axis=0, keepdims=True)[None]
    zb_ref[...] = proj(6)

    qvt = _dot_nt(wqvt_ref[...], h)
    qt_ref[0] = (qvt[0:D_ATT] * (HEAD_DIM ** -0.5 * LOG2E)).astype(_BF16)
    vt = qvt[D_ATT:2 * D_ATT].astype(_BF16)
    ones = jnp.ones((BF16_ROWS, ROWS_IN), _BF16)
    for hd in range(ATT_HEADS):
        vt_ref[0, hd * VT_ROWS:hd * VT_ROWS + HEAD_DIM, :] = vt[hd * HEAD_DIM:(hd + 1) * HEAD_DIM, :]
        vt_ref[0, hd * VT_ROWS + HEAD_DIM:(hd + 1) * VT_ROWS, :] = ones


def _in_proj(x2d, g, w_bf, wqvt_bf, lng, lnb, ws_pairs, bs_full):
    n = x2d.shape[0]
    nblk = n // ROWS_IN
    row = lambda i: (i, 0)
    const2 = lambda i: (0, 0)
    seg_spec = pl.BlockSpec((ROWS_IN, D_ATT), row)
    return pl.pallas_call(
        _in_proj_kernel,
        grid=(nblk,),
        in_specs=[
            pl.BlockSpec((ROWS_IN, D_MODEL), row),
            pl.BlockSpec((1, D_MODEL), const2),
            pl.BlockSpec((D_MODEL, D_IN), const2),
            pl.BlockSpec((2 * D_ATT, D_MODEL), const2),
            pl.BlockSpec((1, D_SGU), const2),
            pl.BlockSpec((1, D_SGU), const2),
            pl.BlockSpec((D_SGU // LANES, SGU_CHUNK, 2 * SGU_CHUNK), lambda i: (0, 0, 0)),
            pl.BlockSpec((SGU_CHUNK, D_SGU), const2),
        ],
        out_specs=[
            seg_spec,
            pl.BlockSpec((1, D_ATT, ROWS_IN), lambda i: (i, 0, 0)),
            seg_spec,
            pl.BlockSpec((1, ATT_HEADS * VT_ROWS, ROWS_IN), lambda i: (i, 0, 0)),
            seg_spec,
            pl.BlockSpec((1, 1, D_ATT), lambda i: (i, 0, 0)),
        ],
        out_shape=[
            jax.ShapeDtypeStruct((n, D_SGU), _BF16),
            jax.ShapeDtypeStruct((nblk, D_ATT, ROWS_IN), _BF16),
            jax.ShapeDtypeStruct((n, D_ATT), _BF16),
            jax.ShapeDtypeStruct((nblk, ATT_HEADS * VT_ROWS, ROWS_IN), _BF16),
            jax.ShapeDtypeStruct((n, D_ATT), _F32),
            jax.ShapeDtypeStruct((nblk, 1, D_ATT), _F32),
        ],
        compiler_params=pltpu.CompilerParams(
            dimension_semantics=("arbitrary",), vmem_limit_bytes=VMEM_LIMIT),
        name="in_proj",
    )(x2d, g, w_bf, wqvt_bf, lng, lnb, ws_pairs, bs_full)


def _toeplitz(row_ref, hh, t):
    wide = jnp.broadcast_to(row_ref[hh, t], (MOBA_BLOCK, 2 * MOBA_BLOCK))
    return pltpu.roll(wide, 0, 1, stride=1, stride_axis=0)[:, 0:MOBA_BLOCK]


def _moba_kernel(qt_ref, k_ref, ohk_ref, vt_ref, ksum_ref, zb_ref, brow_ref, far_ref,
                 yb_ref, qaug_ref, qfar_ref, bias_ref, m_ref, acc_ref,
                 sa_ref, sb_ref, ma_ref, mb_ref):
    i = pl.program_id(2)
    blk = MOBA_BLOCK
    nq = ksum_ref.shape[0]

    @pl.when(i == 0)
    def _():
        for hh in range(HEADS_PER_STEP):
            for t in range(2):
                bias_ref[t * blk:(t + 1) * blk, hh * blk:(hh + 1) * blk] = _toeplitz(brow_ref, hh, t)

    qf = qt_ref[...].astype(_F32)
    ch = lax.broadcasted_iota(jnp.int32, qf.shape, 0)
    qh16 = jnp.concatenate([jnp.where(ch < HEAD_DIM, qf, 0.0),
                            jnp.where(ch < HEAD_DIM, 0.0, qf)], axis=1).astype(_BF16)
    kmean = ksum_ref[...] * (1.0 / blk)
    g = None
    for _ in range(3):
        part = kmean.astype(_BF16)
        kmean = kmean - part.astype(_F32)
        term = _dot(part, qh16)
        g = term if g is None else g + term
    row = lax.broadcasted_iota(jnp.int32, (nq, STACK), 0)
    row_f = row.astype(_F32)
    g = jnp.where(row < i, g, NEG)
    sel = jnp.zeros((nq, STACK), jnp.bool_)
    for r in range(MOBA_TOPK):
        top = jnp.max(g, axis=0, keepdims=True)
        idx = jnp.min(jnp.where(g == top, row_f, float(nq)), axis=0, keepdims=True)
        pick = row_f == idx
        sel = sel | (pick & (jnp.full_like(row, r) < i))
        g = jnp.where(pick, -3e38, g)
    selb = jnp.where(sel, far_ref[...], NEG)
    pad = jnp.zeros((LANES - nq, STACK), _BF16)
    qaug_ref[0:LANES] = qh16
    qaug_ref[LANES:LANES + nq] = selb.astype(_BF16)
    qaug_ref[LANES + nq:2 * LANES] = pad
    qfar_ref[0:LANES] = qh16
    qfar_ref[LANES:LANES + nq] = jnp.where(row == i - 1, NEG, selb).astype(_BF16)
    qfar_ref[LANES + nq:2 * LANES] = pad

    m_ref[...] = jnp.full(m_ref.shape, NEG, _F32)
    acc_ref[...] = jnp.zeros(acc_ref.shape, _F32)

    def key_rows(j0, nblk):
        return pl.ds(pl.multiple_of(j0 * blk, blk), nblk * blk)

    def produce(buf, s):
        s_ref, smax_ref = buf
        s_ref[0:s.shape[0]] = s
        smax_ref[...] = jnp.max(s, axis=0, keepdims=True)

    def online_update(buf, blocks):
        s_ref, smax_ref = buf
        s = s_ref[0:len(blocks) * blk]
        m_prev = m_ref[...]
        m_new = jnp.maximum(m_prev, smax_ref[...])
        alpha = jnp.exp2(m_prev - m_new)
        p = jnp.exp2(s - m_new).astype(_BF16)
        for hh in range(HEADS_PER_STEP):
            cols = slice(hh * blk, (hh + 1) * blk)
            vrows = slice(hh * VT_ROWS, (hh + 1) * VT_ROWS)
            o = _dot(vt_ref[blocks[0], vrows, :], p[0:blk, cols])
            for n_, b_ in enumerate(blocks[1:], start=1):
                o = o + _dot(vt_ref[b_, vrows, :], p[n_ * blk:(n_ + 1) * blk, cols])
            acc_ref[hh] = alpha[:, cols] * acc_ref[hh] + o
        m_ref[...] = m_new

    def near_scores():
        jp = jnp.maximum(i - 1, 0)
        own = _dot(k_ref[key_rows(i, 1), :], qh16)
        prev = _dot(jnp.concatenate([k_ref[key_rows(jp, 1), :], ohk_ref[key_rows(jp, 1), :]], axis=1),
                    qaug_ref[...])
        return jnp.concatenate([own, prev], axis=0) + bias_ref[...]

    def far_scores(hf):
        rows = key_rows(hf * HALF, HALF)
        return _dot(jnp.concatenate([k_ref[rows, :], ohk_ref[rows, :]], axis=1), qfar_ref[...])

    def far_blocks(hf):
        return tuple(hf * HALF + b_ for b_ in range(HALF))

    near = (i, jnp.maximum(i - 1, 0))
    n_far = jnp.maximum(i - 1, 0)
    n_half = lax.shift_right_logical(n_far + (HALF - 1), HALF_LOG2)
    buf_a = (sa_ref, ma_ref)
    buf_b = (sb_ref, mb_ref)

    produce(buf_a, near_scores())

    @pl.when(n_half == 0)
    def _():
        online_update(buf_a, near)

    @pl.when(n_half >= 1)
    def _():
        produce(buf_b, far_scores(0))
        online_update(buf_a, near)
        n_pairs = lax.shift_right_logical(n_half - 1, 1)

        def far_step(t, carry):
            hf = 2 * t
            produce(buf_a, far_scores(hf + 1))
            online_update(buf_b, far_blocks(hf))
            produce(buf_b, far_scores(hf + 2))
            online_update(buf_a, far_blocks(hf + 1))
            return carry

        lax.fori_loop(0, n_pairs, far_step, 0)
        hl = 2 * n_pairs

        @pl.when(n_half - hl == 2)
        def _():
            produce(buf_a, far_scores(hl + 1))
            online_update(buf_b, far_blocks(hl))
            online_update(buf_a, far_blocks(hl + 1))

        @pl.when(n_half - hl == 1)
        def _():
            online_update(buf_b, far_blocks(hl))

    o_t = jnp.concatenate(
        [acc_ref[hh, 0:HEAD_DIM, :] / acc_ref[hh, HEAD_DIM:HEAD_DIM + 1, :]
         for hh in range(HEADS_PER_STEP)], axis=0)
    yb_ref[...] = (o_t.T * jax.nn.silu(zb_ref[...])).astype(_BF16)


def _moba(qt, k, ohk, vt, ksum, zb, bias_rows, far_rows, batch, seq):
    n = k.shape[0]
    nq = seq // MOBA_BLOCK
    tile = pl.BlockSpec((MOBA_BLOCK, LANES), lambda b, hp, i: (b * nq + i, hp))
    return pl.pallas_call(
        _moba_kernel,
        grid=(batch, HEAD_PAIRS, nq),
        in_specs=[
            pl.BlockSpec((None, LANES, MOBA_BLOCK), lambda b, hp, i: (b * nq + i, hp, 0)),
            pl.BlockSpec((seq, LANES), lambda b, hp, i: (b, hp)),
            pl.BlockSpec((seq, LANES), lambda b, hp, i: (0, 0)),
            pl.BlockSpec((nq, HEADS_PER_STEP * VT_ROWS, MOBA_BLOCK), lambda b, hp, i: (b, hp, 0)),
            pl.BlockSpec((nq, LANES), lambda b, hp, i: (b, hp)),
            tile,
            pl.BlockSpec((HEADS_PER_STEP, 2, 1, 2 * MOBA_BLOCK), lambda b, hp, i: (hp, 0, 0, 0)),
            pl.BlockSpec((None, 1, STACK), lambda b, hp, i: (hp, 0, 0)),
        ],
        out_specs=tile,
        out_shape=jax.ShapeDtypeStruct((n, D_ATT), _BF16),
        scratch_shapes=[
            pltpu.VMEM((2 * LANES, STACK), _BF16),
            pltpu.VMEM((2 * LANES, STACK), _BF16),
            pltpu.VMEM((NEAR_ROWS, STACK), _F32),
            pltpu.VMEM((1, STACK), _F32),
            pltpu.VMEM((HEADS_PER_STEP, VT_ROWS, MOBA_BLOCK), _F32),
            pltpu.VMEM((HALF_ROWS, STACK), _F32),
            pltpu.VMEM((HALF_ROWS, STACK), _F32),
            pltpu.VMEM((1, STACK), _F32),
            pltpu.VMEM((1, STACK), _F32),
        ],
        compiler_params=pltpu.CompilerParams(
            dimension_semantics=("arbitrary", "arbitrary", "arbitrary"),
            vmem_limit_bytes=VMEM_LIMIT),
        name="moba",
    )(qt, k, ohk, vt, ksum, zb, bias_rows, far_rows)


def _out_proj_kernel(ya_ref, yb_ref, x_ref, w_ref, g_ref, o_ref, *, final):
    y = (x_ref[...] + _dot(ya_ref[...], w_ref[0:D_SGU, :])
         + _dot(yb_ref[...], w_ref[D_SGU:D_SGU + D_ATT, :]))
    if final:
        ms = jnp.mean(y * y, axis=-1, keepdims=True)
        y = y * lax.rsqrt(ms + NORM_EPS) * g_ref[...]
    o_ref[...] = y


def _out_proj(ya, yb, x2d, w_bf, final_g, final):
    n = x2d.shape[0]
    row = lambda i: (i, 0)
    const2 = lambda i: (0, 0)
    return pl.pallas_call(
        functools.partial(_out_proj_kernel, final=final),
        grid=(n // ROWS_OUT,),
        in_specs=[
            pl.BlockSpec((ROWS_OUT, D_SGU), row),
            pl.BlockSpec((ROWS_OUT, D_ATT), row),
            pl.BlockSpec((ROWS_OUT, D_MODEL), row),
            pl.BlockSpec((D_SGU + D_ATT, D_MODEL), const2),
            pl.BlockSpec((1, D_MODEL), const2),
        ],
        out_specs=pl.BlockSpec((ROWS_OUT, D_MODEL), row),
        out_shape=jax.ShapeDtypeStruct((n, D_MODEL), _F32),
        compiler_params=pltpu.CompilerParams(
            dimension_semantics=("arbitrary",), vmem_limit_bytes=VMEM_LIMIT),
        name="out_proj",
    )(ya, yb, x2d, w_bf, final_g)


def _rel_bucket(dist):
    n = jnp.maximum(dist, 0)
    max_exact = REL_BUCKETS // 2
    nf = jnp.maximum(n, 1).astype(_F32)
    large = max_exact + (jnp.log(nf / max_exact) / math.log(REL_MAX_DIST / max_exact)
                         * (REL_BUCKETS - max_exact)).astype(jnp.int32)
    large = jnp.minimum(large, REL_BUCKETS - 1)
    return jnp.where(n < max_exact, n, large)


def _bias_tables(rel_bias):
    table = rel_bias.T.astype(_F32) * LOG2E
    period = 2 * MOBA_BLOCK
    u = jnp.arange(period, dtype=jnp.int32)
    far = table[:, REL_BUCKETS - 1]
    own = jnp.where((u < MOBA_BLOCK)[None], table[:, _rel_bucket(u)], NEG)
    prev = table[:, _rel_bucket((u + MOBA_BLOCK) % period)] - far[:, None]
    rows = jnp.stack([own, prev], axis=1)[:, :, None, :]
    far_rows = jnp.repeat(far, MOBA_BLOCK).reshape(HEAD_PAIRS, 1, STACK)
    return rows, far_rows


def _sgu_tables(w_s, b_s):
    causal = jnp.tril(jnp.ones((SGU_CHUNK, SGU_CHUNK), dtype=bool))
    w = jnp.where(causal[None], w_s, 0.0).astype(_BF16)
    pairs = w.reshape(SGU_GROUPS // 2, 2, SGU_CHUNK, SGU_CHUNK).transpose(0, 2, 1, 3)
    pairs = pairs.reshape(SGU_GROUPS // 2, SGU_CHUNK, 2 * SGU_CHUNK)
    bs_full = jnp.repeat(b_s.T.astype(_F32), SGU_GROUP_DIM, axis=1)
    return pairs, bs_full


def kernel(x, norm_g, w_in, sgu_ln_g, sgu_ln_b, sgu_w, sgu_b, w_out, rel_bias, final_g):
    batch, seq, _ = x.shape
    depth = norm_g.shape[0]
    nq = seq // MOBA_BLOCK
    assert seq % MOBA_BLOCK == 0 and nq % HALF == 0 and nq % BF16_ROWS == 0 and nq <= LANES
    x2d = x.reshape(batch * seq, D_MODEL)
    bias_rows, far_rows = _bias_tables(rel_bias)
    ohk = (jnp.arange(seq, dtype=jnp.int32)[:, None] // MOBA_BLOCK
           == jnp.arange(LANES, dtype=jnp.int32)[None, :]).astype(_BF16)
    fg = final_g.reshape(1, D_MODEL)
    for l in range(depth):
        ws_pairs, bs_full = _sgu_tables(sgu_w[l], sgu_b[l])
        w_bf = w_in[l].astype(_BF16)
        wqvt_bf = jnp.concatenate([w_bf[:, Q_SEG * D_SGU:(Q_SEG + 1) * D_SGU],
                                   w_bf[:, V_SEG * D_SGU:(V_SEG + 1) * D_SGU]], axis=1).T
        ya, qt, k, vt, zb, ksum = _in_proj(
            x2d, norm_g[l].reshape(1, D_MODEL), w_bf, wqvt_bf,
            sgu_ln_g[l].reshape(1, D_SGU), sgu_ln_b[l].reshape(1, D_SGU), ws_pairs, bs_full)
        yb = _moba(qt, k, ohk, vt, ksum.reshape(batch * nq, D_ATT), zb,
                   bias_rows, far_rows, batch, seq)
        x2d = _out_proj(ya, yb, x2d, w_out[l].astype(_BF16), fg, final=(l == depth - 1))
    return x2d.reshape(batch, seq, D_MODEL)
```

```python
import functools
import math

import jax
import jax.numpy as jnp
from jax import lax
from jax.experimental import pallas as pl
from jax.experimental.pallas import tpu as pltpu

D_MODEL = 1024
D_SGU = 512
D_ATT = 512
SGU_GROUPS = 8
SGU_GROUP_DIM = D_SGU // SGU_GROUPS
SGU_CHUNK = 128
ATT_HEADS = 8
HEAD_DIM = D_ATT // ATT_HEADS
MOBA_BLOCK = 256
MOBA_TOPK = 3
REL_BUCKETS = 32
REL_MAX_DIST = 128
NORM_EPS = 1e-6
LN_EPS = 1e-5
NEG = -1e30
LOG2E = math.log2(math.e)
D_IN = 3 * D_SGU + 4 * D_ATT
Q_SEG, V_SEG = 3, 5

LANES = 128
BF16_ROWS = 16
HEADS_PER_STEP = LANES // HEAD_DIM
HEAD_PAIRS = ATT_HEADS // HEADS_PER_STEP
STACK = HEADS_PER_STEP * MOBA_BLOCK
VT_ROWS = HEAD_DIM + BF16_ROWS
HALF_LOG2 = 2
HALF = 1 << HALF_LOG2
HALF_ROWS = HALF * MOBA_BLOCK
NEAR_ROWS = 2 * MOBA_BLOCK
TILES_PER_STEP = 2
ROWS_IN = 256
ROWS_OUT = 512
VMEM_LIMIT = 48 * 1024 * 1024

_BF16 = jnp.bfloat16
_F32 = jnp.float32


def _dot(a, b):
    return jnp.dot(a, b, preferred_element_type=_F32)


def _dot_nt(a, b):
    return lax.dot_general(a, b, (((1,), (1,)), ((), ())), preferred_element_type=_F32)


def _in_proj_kernel(x_ref, g_ref, w_ref, wqvt_ref, lng_ref, lnb_ref, ws_ref, bs_ref,
                    ya_ref, qt_ref, k_ref, vt_ref, zb_ref, ksum_ref):
    x = x_ref[...]
    ms = jnp.mean(x * x, axis=-1, keepdims=True)
    h = (x * lax.rsqrt(ms + NORM_EPS) * g_ref[...]).astype(_BF16)

    def proj(seg):
        return _dot(h, w_ref[:, seg * D_SGU:(seg + 1) * D_SGU])

    gv = jax.nn.gelu(proj(1))
    mu = jnp.mean(gv, axis=-1, keepdims=True)
    var = jnp.mean(jnp.square(gv - mu), axis=-1, keepdims=True)
    vln = (gv - mu) * lax.rsqrt(var + LN_EPS) * lng_ref[...] + lnb_ref[...]

    gate = jax.nn.gelu(proj(0)) * jax.nn.silu(proj(2))

    lane = lax.broadcasted_iota(jnp.int32, (SGU_CHUNK, LANES), 1)
    low = lane < SGU_GROUP_DIM
    for c in range(ROWS_IN // SGU_CHUNK):
        rows = slice(c * SGU_CHUNK, (c + 1) * SGU_CHUNK)
        for p in range(D_SGU // LANES):
            cols = slice(p * LANES, (p + 1) * LANES)
            vp = vln[rows, cols]
            rhs = jnp.concatenate([jnp.where(low, vp, 0.0).astype(_BF16),
                                   jnp.where(low, 0.0, vp).astype(_BF16)], axis=0)
            mixed = _dot(ws_ref[p], rhs) + bs_ref[:, cols]
            ya_ref[rows, cols] = (gate[rows, cols] * mixed).astype(_BF16)

    k = proj(4)
    k_ref[...] = k.astype(_BF16)
    ksum_ref[...] = jnp.sum(k, axis=0, keepdims=True)[None]
    zb_ref[...] = proj(6)

    qvt = _dot_nt(wqvt_ref[...], h)
    qt_ref[0] = (qvt[0:D_ATT] * (HEAD_DIM ** -0.5 * LOG2E)).astype(_BF16)
    vt = qvt[D_ATT:2 * D_ATT].astype(_BF16)
    ones = jnp.ones((BF16_ROWS, ROWS_IN), _BF16)
    for hd in range(ATT_HEADS):
        vt_ref[0, hd * VT_ROWS:hd * VT_ROWS + HEAD_DIM, :] = vt[hd * HEAD_DIM:(hd + 1) * HEAD_DIM, :]
        vt_ref[0, hd * VT_ROWS + HEAD_DIM:(hd + 1) * VT_ROWS, :] = ones


def _in_proj(x2d, g, w_bf, wqvt_bf, lng, lnb, ws_pairs, bs_full):
    n = x2d.shape[0]
    nblk = n // ROWS_IN
    row = lambda i: (i, 0)
    const2 = lambda i: (0, 0)
    seg_spec = pl.BlockSpec((ROWS_IN, D_ATT), row)
    return pl.pallas_call(
        _in_proj_kernel,
        grid=(nblk,),
        in_specs=[
            pl.BlockSpec((ROWS_IN, D_MODEL), row),
            pl.BlockSpec((1, D_MODEL), const2),
            pl.BlockSpec((D_MODEL, D_IN), const2),
            pl.BlockSpec((2 * D_ATT, D_MODEL), const2),
            pl.BlockSpec((1, D_SGU), const2),
            pl.BlockSpec((1, D_SGU), const2),
            pl.BlockSpec((D_SGU // LANES, SGU_CHUNK, 2 * SGU_CHUNK), lambda i: (0, 0, 0)),
            pl.BlockSpec((SGU_CHUNK, D_SGU), const2),
        ],
        out_specs=[
            seg_spec,
            pl.BlockSpec((1, D_ATT, ROWS_IN), lambda i: (i, 0, 0)),
            seg_spec,
            pl.BlockSpec((1, ATT_HEADS * VT_ROWS, ROWS_IN), lambda i: (i, 0, 0)),
            seg_spec,
            pl.BlockSpec((1, 1, D_ATT), lambda i: (i, 0, 0)),
        ],
        out_shape=[
            jax.ShapeDtypeStruct((n, D_SGU), _BF16),
            jax.ShapeDtypeStruct((nblk, D_ATT, ROWS_IN), _BF16),
            jax.ShapeDtypeStruct((n, D_ATT), _BF16),
            jax.ShapeDtypeStruct((nblk, ATT_HEADS * VT_ROWS, ROWS_IN), _BF16),
            jax.ShapeDtypeStruct((n, D_ATT), _F32),
            jax.ShapeDtypeStruct((nblk, 1, D_ATT), _F32),
        ],
        compiler_params=pltpu.CompilerParams(
            dimension_semantics=("arbitrary",), vmem_limit_bytes=VMEM_LIMIT),
        name="in_proj",
    )(x2d, g, w_bf, wqvt_bf, lng, lnb, ws_pairs, bs_full)


def _toeplitz(row_ref, hh, t):
    wide = jnp.broadcast_to(row_ref[hh, t], (MOBA_BLOCK, 2 * MOBA_BLOCK))
    return pltpu.roll(wide, 0, 1, stride=1, stride_axis=0)[:, 0:MOBA_BLOCK]


def _moba_kernel(*refs):
    step = pl.program_id(2)

    def tile(sub, carry):
        _moba_tile(step * TILES_PER_STEP + sub, sub, *refs)
        return carry

    lax.fori_loop(0, TILES_PER_STEP, tile, 0)


def _moba_tile(i, sub, qt_ref, k_ref, ohk_ref, vt_ref, ksum_ref, zb_ref, brow_ref, far_ref,
               yb_ref, qaug_ref, qfar_ref, bias_ref, m_ref, acc_ref,
               sa_ref, sb_ref, ma_ref, mb_ref):
    blk = MOBA_BLOCK
    nq = ksum_ref.shape[0]
    tile_rows = pl.ds(pl.multiple_of(sub * blk, blk), blk)

    @pl.when(i == 0)
    def _():
        for hh in range(HEADS_PER_STEP):
            for t in range(2):
                bias_ref[t * blk:(t + 1) * blk, hh * blk:(hh + 1) * blk] = _toeplitz(brow_ref, hh, t)

    qf = qt_ref[sub].astype(_F32)
    ch = lax.broadcasted_iota(jnp.int32, qf.shape, 0)
    qh16 = jnp.concatenate([jnp.where(ch < HEAD_DIM, qf, 0.0),
                            jnp.where(ch < HEAD_DIM, 0.0, qf)], axis=1).astype(_BF16)
    kmean = ksum_ref[...] * (1.0 / blk)
    g = None
    for _ in range(3):
        part = kmean.astype(_BF16)
        kmean = kmean - part.astype(_F32)
        term = _dot(part, qh16)
        g = term if g is None else g + term
    row = lax.broadcasted_iota(jnp.int32, (nq, STACK), 0)
    row_f = row.astype(_F32)
    g = jnp.where(row < i, g, NEG)
    sel = jnp.zeros((nq, STACK), jnp.bool_)
    for r in range(MOBA_TOPK):
        top = jnp.max(g, axis=0, keepdims=True)
        idx = jnp.min(jnp.where(g == top, row_f, float(nq)), axis=0, keepdims=True)
        pick = row_f == idx
        sel = sel | (pick & (jnp.full_like(row, r) < i))
        g = jnp.where(pick, -3e38, g)
    selb = jnp.where(sel, far_ref[...], NEG)
    pad = jnp.zeros((LANES - nq, STACK), _BF16)
    qaug_ref[0:LANES] = qh16
    qaug_ref[LANES:LANES + nq] = selb.astype(_BF16)
    qaug_ref[LANES + nq:2 * LANES] = pad
    qfar_ref[0:LANES] = qh16
    qfar_ref[LANES:LANES + nq] = jnp.where(row == i - 1, NEG, selb).astype(_BF16)
    qfar_ref[LANES + nq:2 * LANES] = pad

    m_ref[...] = jnp.full(m_ref.shape, NEG, _F32)
    acc_ref[...] = jnp.zeros(acc_ref.shape, _F32)

    def key_rows(j0, nblk):
        return pl.ds(pl.multiple_of(j0 * blk, blk), nblk * blk)

    def produce(buf, s):
        s_ref, smax_ref = buf
        s_ref[0:s.shape[0]] = s
        smax_ref[...] = jnp.max(s, axis=0, keepdims=True)

    def online_update(buf, blocks):
        s_ref, smax_ref = buf
        s = s_ref[0:len(blocks) * blk]
        m_prev = m_ref[...]
        m_new = jnp.maximum(m_prev, smax_ref[...])
        alpha = jnp.exp2(m_prev - m_new)
        p = jnp.exp2(s - m_new).astype(_BF16)
        for hh in range(HEADS_PER_STEP):
            cols = slice(hh * blk, (hh + 1) * blk)
            vrows = slice(hh * VT_ROWS, (hh + 1) * VT_ROWS)
            o = _dot(vt_ref[blocks[0], vrows, :], p[0:blk, cols])
            for n_, b_ in enumerate(blocks[1:], start=1):
                o = o + _dot(vt_ref[b_, vrows, :], p[n_ * blk:(n_ + 1) * blk, cols])
            acc_ref[hh] = alpha[:, cols] * acc_ref[hh] + o
        m_ref[...] = m_new

    def near_scores():
        jp = jnp.maximum(i - 1, 0)
        own = _dot(k_ref[key_rows(i, 1), :], qh16)
        prev = _dot(jnp.concatenate([k_ref[key_rows(jp, 1), :], ohk_ref[key_rows(jp, 1), :]], axis=1),
                    qaug_ref[...])
        return jnp.concatenate([own, prev], axis=0) + bias_ref[...]

    def far_scores(hf):
        rows = key_rows(hf * HALF, HALF)
        return _dot(jnp.concatenate([k_ref[rows, :], ohk_ref[rows, :]], axis=1), qfar_ref[...])

    def far_blocks(hf):
        return tuple(hf * HALF + b_ for b_ in range(HALF))

    near = (i, jnp.maximum(i - 1, 0))
    n_far = jnp.maximum(i - 1, 0)
    n_half = lax.shift_right_logical(n_far + (HALF - 1), HALF_LOG2)
    buf_a = (sa_ref, ma_ref)
    buf_b = (sb_ref, mb_ref)

    produce(buf_a, near_scores())

    @pl.when(n_half == 0)
    def _():
        online_update(buf_a, near)

    @pl.when(n_half >= 1)
    def _():
        produce(buf_b, far_scores(0))
        online_update(buf_a, near)
        n_pairs = lax.shift_right_logical(n_half - 1, 1)

        def far_step(t, carry):
            hf = 2 * t
            produce(buf_a, far_scores(hf + 1))
            online_update(buf_b, far_blocks(hf))
            produce(buf_b, far_scores(hf + 2))
            online_update(buf_a, far_blocks(hf + 1))
            return carry

        lax.fori_loop(0, n_pairs, far_step, 0)
        hl = 2 * n_pairs

        @pl.when(n_half - hl == 2)
        def _():
            produce(buf_a, far_scores(hl + 1))
            online_update(buf_b, far_blocks(hl))
            online_update(buf_a, far_blocks(hl + 1))

        @pl.when(n_half - hl == 1)
        def _():
            online_update(buf_b, far_blocks(hl))

    o_t = jnp.concatenate(
        [acc_ref[hh, 0:HEAD_DIM, :] / acc_ref[hh, HEAD_DIM:HEAD_DIM + 1, :]
         for hh in range(HEADS_PER_STEP)], axis=0)
    yb_ref[tile_rows, :] = (o_t.T * jax.nn.silu(zb_ref[tile_rows, :])).astype(_BF16)


def _moba(qt, k, ohk, vt, ksum, zb, bias_rows, far_rows, batch, seq):
    n = k.shape[0]
    nq = seq // MOBA_BLOCK
    steps = nq // TILES_PER_STEP
    tile = pl.BlockSpec((TILES_PER_STEP * MOBA_BLOCK, LANES), lambda b, hp, i: (b * steps + i, hp))
    return pl.pallas_call(
        _moba_kernel,
        grid=(batch, HEAD_PAIRS, steps),
        in_specs=[
            pl.BlockSpec((TILES_PER_STEP, LANES, MOBA_BLOCK), lambda b, hp, i: (b * steps + i, hp, 0)),
            pl.BlockSpec((seq, LANES), lambda b, hp, i: (b, hp)),
            pl.BlockSpec((seq, LANES), lambda b, hp, i: (0, 0)),
            pl.BlockSpec((nq, HEADS_PER_STEP * VT_ROWS, MOBA_BLOCK), lambda b, hp, i: (b, hp, 0)),
            pl.BlockSpec((nq, LANES), lambda b, hp, i: (b, hp)),
            tile,
            pl.BlockSpec((HEADS_PER_STEP, 2, 1, 2 * MOBA_BLOCK), lambda b, hp, i: (hp, 0, 0, 0)),
            pl.BlockSpec((None, 1, STACK), lambda b, hp, i: (hp, 0, 0)),
        ],
        out_specs=tile,
        out_shape=jax.ShapeDtypeStruct((n, D_ATT), _BF16),
        scratch_shapes=[
            pltpu.VMEM((2 * LANES, STACK), _BF16),
            pltpu.VMEM((2 * LANES, STACK), _BF16),
            pltpu.VMEM((NEAR_ROWS, STACK), _F32),
            pltpu.VMEM((1, STACK), _F32),
            pltpu.VMEM((HEADS_PER_STEP, VT_ROWS, MOBA_BLOCK), _F32),
            pltpu.VMEM((HALF_ROWS, STACK), _F32),
            pltpu.VMEM((HALF_ROWS, STACK), _F32),
            pltpu.VMEM((1, STACK), _F32),
            pltpu.VMEM((1, STACK), _F32),
        ],
        compiler_params=pltpu.CompilerParams(
            dimension_semantics=("arbitrary", "arbitrary", "arbitrary"),
            vmem_limit_bytes=VMEM_LIMIT),
        name="moba",
    )(qt, k, ohk, vt, ksum, zb, bias_rows, far_rows)


def _out_proj_kernel(ya_ref, yb_ref, x_ref, w_ref, g_ref, o_ref, *, final):
    y = (x_ref[...] + _dot(ya_ref[...], w_ref[0:D_SGU, :])
         + _dot(yb_ref[...], w_ref[D_SGU:D_SGU + D_ATT, :]))
    if final:
        ms = jnp.mean(y * y, axis=-1, keepdims=True)
        y = y * lax.rsqrt(ms + NORM_EPS) * g_ref[...]
    o_ref[...] = y


def _out_proj(ya, yb, x2d, w_bf, final_g, final):
    n = x2d.shape[0]
    row = lambda i: (i, 0)
    const2 = lambda i: (0, 0)
    return pl.pallas_call(
        functools.partial(_out_proj_kernel, final=final),
        grid=(n // ROWS_OUT,),
        in_specs=[
            pl.BlockSpec((ROWS_OUT, D_SGU), row),
            pl.BlockSpec((ROWS_OUT, D_ATT), row),
            pl.BlockSpec((ROWS_OUT, D_MODEL), row),
            pl.BlockSpec((D_SGU + D_ATT, D_MODEL), const2),
            pl.BlockSpec((1, D_MODEL), const2),
        ],
        out_specs=pl.BlockSpec((ROWS_OUT, D_MODEL), row),
        out_shape=jax.ShapeDtypeStruct((n, D_MODEL), _F32),
        compiler_params=pltpu.CompilerParams(
            dimension_semantics=("arbitrary",), vmem_limit_bytes=VMEM_LIMIT),
        name="out_proj",
    )(ya, yb, x2d, w_bf, final_g)


def _rel_bucket(dist):
    n = jnp.maximum(dist, 0)
    max_exact = REL_BUCKETS // 2
    nf = jnp.maximum(n, 1).astype(_F32)
    large = max_exact + (jnp.log(nf / max_exact) / math.log(REL_MAX_DIST / max_exact)
                         * (REL_BUCKETS - max_exact)).astype(jnp.int32)
    large = jnp.minimum(large, REL_BUCKETS - 1)
    return jnp.where(n < max_exact, n, large)


def _bias_tables(rel_bias):
    table = rel_bias.T.astype(_F32) * LOG2E
    period = 2 * MOBA_BLOCK
    u = jnp.arange(period, dtype=jnp.int32)
    far = table[:, REL_BUCKETS - 1]
    own = jnp.where((u < MOBA_BLOCK)[None], table[:, _rel_bucket(u)], NEG)
    prev = table[:, _rel_bucket((u + MOBA_BLOCK) % period)] - far[:, None]
    rows = jnp.stack([own, prev], axis=1)[:, :, None, :]
    far_rows = jnp.repeat(far, MOBA_BLOCK).reshape(HEAD_PAIRS, 1, STACK)
    return rows, far_rows


def _sgu_tables(w_s, b_s):
    causal = jnp.tril(jnp.ones((SGU_CHUNK, SGU_CHUNK), dtype=bool))
    w = jnp.where(causal[None], w_s, 0.0).astype(_BF16)
    pairs = w.reshape(SGU_GROUPS // 2, 2, SGU_CHUNK, SGU_CHUNK).transpose(0, 2, 1, 3)
    pairs = pairs.reshape(SGU_GROUPS // 2, SGU_CHUNK, 2 * SGU_CHUNK)
    bs_full = jnp.repeat(b_s.T.astype(_F32), SGU_GROUP_DIM, axis=1)
    return pairs, bs_full


def kernel(x, norm_g, w_in, sgu_ln_g, sgu_ln_b, sgu_w, sgu_b, w_out, rel_bias, final_g):
    batch, seq, _ = x.shape
    depth = norm_g.shape[0]
    nq = seq // MOBA_BLOCK
    assert seq % MOBA_BLOCK == 0 and nq % HALF == 0 and nq % BF16_ROWS == 0 and nq <= LANES
    assert nq % TILES_PER_STEP == 0
    x2d = x.reshape(batch * seq, D_MODEL)
    bias_rows, far_rows = _bias_tables(rel_bias)
    ohk = (jnp.arange(seq, dtype=jnp.int32)[:, None] // MOBA_BLOCK
           == jnp.arange(LANES, dtype=jnp.int32)[None, :]).astype(_BF16)
    fg = final_g.reshape(1, D_MODEL)
    for l in range(depth):
        ws_pairs, bs_full = _sgu_tables(sgu_w[l], sgu_b[l])
        w_bf = w_in[l].astype(_BF16)
        wqvt_bf = jnp.concatenate([w_bf[:, Q_SEG * D_SGU:(Q_SEG + 1) * D_SGU],
                                   w_bf[:, V_SEG * D_SGU:(V_SEG + 1) * D_SGU]], axis=1).T
        ya, qt, k, vt, zb, ksum = _in_proj(
            x2d, norm_g[l].reshape(1, D_MODEL), w_bf, wqvt_bf,
            sgu_ln_g[l].reshape(1, D_SGU), sgu_ln_b[l].reshape(1, D_SGU), ws_pairs, bs_full)
        yb = _moba(qt, k, ohk, vt, ksum.reshape(batch * nq, D_ATT), zb,
                   bias_rows, far_rows, batch, seq)
        x2d = _out_proj(ya, yb, x2d, w_out[l].astype(_BF16), fg, final=(l == depth - 1))
    return x2d.reshape(batch, seq, D_MODEL)
```

```python
import functools
import math

import jax
import jax.numpy as jnp
from jax import lax
from jax.experimental import pallas as pl
from jax.experimental.pallas import tpu as pltpu

D_MODEL = 1024
D_SGU = 512
D_ATT = 512
SGU_GROUPS = 8
SGU_GROUP_DIM = D_SGU // SGU_GROUPS
SGU_CHUNK = 128
ATT_HEADS = 8
HEAD_DIM = D_ATT // ATT_HEADS
MOBA_BLOCK = 256
MOBA_TOPK = 3
REL_BUCKETS = 32
REL_MAX_DIST = 128
NORM_EPS = 1e-6
LN_EPS = 1e-5
NEG = -1e30
LOG2E = math.log2(math.e)
D_IN = 3 * D_SGU + 4 * D_ATT
Q_SEG, V_SEG = 3, 5

LANES = 128
BF16_ROWS = 16
HEADS_PER_STEP = LANES // HEAD_DIM
HEAD_PAIRS = ATT_HEADS // HEADS_PER_STEP
STACK = HEADS_PER_STEP * MOBA_BLOCK
VT_ROWS = HEAD_DIM + BF16_ROWS
HALF_LOG2 = 2
HALF = 1 << HALF_LOG2
HALF_ROWS = HALF * MOBA_BLOCK
NEAR_ROWS = 2 * MOBA_BLOCK
ROWS_IN = 256
ROWS_OUT = 512
VMEM_LIMIT = 48 * 1024 * 1024

_BF16 = jnp.bfloat16
_F32 = jnp.float32


def _dot(a, b):
    return jnp.dot(a, b, preferred_element_type=_F32)


def _dot_nt(a, b):
    return lax.dot_general(a, b, (((1,), (1,)), ((), ())), preferred_element_type=_F32)


def _in_proj_kernel(x_ref, g_ref, w_ref, wqvt_ref, lng_ref, lnb_ref, ws_ref, bs_ref,
                    ya_ref, qt_ref, k_ref, vt_ref, zb_ref, ksum_ref):
    x = x_ref[...]
    ms = jnp.mean(x * x, axis=-1, keepdims=True)
    h = (x * lax.rsqrt(ms + NORM_EPS) * g_ref[...]).astype(_BF16)

    def proj(seg):
        return _dot(h, w_ref[:, seg * D_SGU:(seg + 1) * D_SGU])

    gv = jax.nn.gelu(proj(1))
    mu = jnp.mean(gv, axis=-1, keepdims=True)
    var = jnp.mean(jnp.square(gv - mu), axis=-1, keepdims=True)
    vln = (gv - mu) * lax.rsqrt(var + LN_EPS) * lng_ref[...] + lnb_ref[...]

    gate = jax.nn.gelu(proj(0)) * jax.nn.silu(proj(2))

    lane = lax.broadcasted_iota(jnp.int32, (SGU_CHUNK, LANES), 1)
    low = lane < SGU_GROUP_DIM
    for c in range(ROWS_IN // SGU_CHUNK):
        rows = slice(c * SGU_CHUNK, (c + 1) * SGU_CHUNK)
        for p in range(D_SGU // LANES):
            cols = slice(p * LANES, (p + 1) * LANES)
            vp = vln[rows, cols]
            rhs = jnp.concatenate([jnp.where(low, vp, 0.0).astype(_BF16),
                                   jnp.where(low, 0.0, vp).astype(_BF16)], axis=0)
            mixed = _dot(ws_ref[p], rhs) + bs_ref[:, cols]
            ya_ref[rows, cols] = (gate[rows, cols] * mixed).astype(_BF16)

    k = proj(4)
    k_ref[...] = k.astype(_BF16)
    ksum_ref[...] = jnp.sum(k, axis=0, keepdims=True)[None]
    zb_ref[...] = proj(6)

    qvt = _dot_nt(wqvt_ref[...], h)
    qt_ref[0] = (qvt[0:D_ATT] * (HEAD_DIM ** -0.5 * LOG2E)).astype(_BF16)
    vt = qvt[D_ATT:2 * D_ATT].astype(_BF16)
    ones = jnp.ones((BF16_ROWS, ROWS_IN), _BF16)
    for hd in range(ATT_HEADS):
        vt_ref[0, hd * VT_ROWS:hd * VT_ROWS + HEAD_DIM, :] = vt[hd * HEAD_DIM:(hd + 1) * HEAD_DIM, :]
        vt_ref[0, hd * VT_ROWS + HEAD_DIM:(hd + 1) * VT_ROWS, :] = ones


def _in_proj(x2d, g, w_bf, wqvt_bf, lng, lnb, ws_pairs, bs_full):
    n = x2d.shape[0]
    nblk = n // ROWS_IN
    row = lambda i: (i, 0)
    const2 = lambda i: (0, 0)
    seg_spec = pl.BlockSpec((ROWS_IN, D_ATT), row)
    return pl.pallas_call(
        _in_proj_kernel,
        grid=(nblk,),
        in_specs=[
            pl.BlockSpec((ROWS_IN, D_MODEL), row),
            pl.BlockSpec((1, D_MODEL), const2),
            pl.BlockSpec((D_MODEL, D_IN), const2),
            pl.BlockSpec((2 * D_ATT, D_MODEL), const2),
            pl.BlockSpec((1, D_SGU), const2),
            pl.BlockSpec((1, D_SGU), const2),
            pl.BlockSpec((D_SGU // LANES, SGU_CHUNK, 2 * SGU_CHUNK), lambda i: (0, 0, 0)),
            pl.BlockSpec((SGU_CHUNK, D_SGU), const2),
        ],
        out_specs=[
            seg_spec,
            pl.BlockSpec((1, D_ATT, ROWS_IN), lambda i: (i, 0, 0)),
            seg_spec,
            pl.BlockSpec((1, ATT_HEADS * VT_ROWS, ROWS_IN), lambda i: (i, 0, 0)),
            seg_spec,
            pl.BlockSpec((1, 1, D_ATT), lambda i: (i, 0, 0)),
        ],
        out_shape=[
            jax.ShapeDtypeStruct((n, D_SGU), _BF16),
            jax.ShapeDtypeStruct((nblk, D_ATT, ROWS_IN), _BF16),
            jax.ShapeDtypeStruct((n, D_ATT), _BF16),
            jax.ShapeDtypeStruct((nblk, ATT_HEADS * VT_ROWS, ROWS_IN), _BF16),
            jax.ShapeDtypeStruct((n, D_ATT), _F32),
            jax.ShapeDtypeStruct((nblk, 1, D_ATT), _F32),
        ],
        compiler_params=pltpu.CompilerParams(
            dimension_semantics=("arbitrary",), vmem_limit_bytes=VMEM_LIMIT),
        name="in_proj",
    )(x2d, g, w_bf, wqvt_bf, lng, lnb, ws_pairs, bs_full)


def _toeplitz(row_ref, hh, t):
    wide = jnp.broadcast_to(row_ref[hh, t], (MOBA_BLOCK, 2 * MOBA_BLOCK))
    return pltpu.roll(wide, 0, 1, stride=1, stride_axis=0)[:, 0:MOBA_BLOCK]


def _moba_kernel(qt_ref, k_ref, ohk_ref, vt_ref, ksum_ref, zb_ref, brow_ref, far_ref,
                 yb_ref, qaug_ref, qfar_ref, bias_ref, m_ref, acc_ref,
                 sa_ref, sb_ref, ma_ref, mb_ref):
    i = pl.program_id(2)
    blk = MOBA_BLOCK
    nq = ksum_ref.shape[0]

    @pl.when(i == 0)
    def _():
        for hh in range(HEADS_PER_STEP):
            for t in range(2):
                bias_ref[t * blk:(t + 1) * blk, hh * blk:(hh + 1) * blk] = _toeplitz(brow_ref, hh, t)

    qf = qt_ref[...].astype(_F32)
    ch = lax.broadcasted_iota(jnp.int32, qf.shape, 0)
    qh16 = jnp.concatenate([jnp.where(ch < HEAD_DIM, qf, 0.0),
                            jnp.where(ch < HEAD_DIM, 0.0, qf)], axis=1).astype(_BF16)
    kmean = ksum_ref[...] * (1.0 / blk)
    g = None
    for _ in range(3):
        part = kmean.astype(_BF16)
        kmean = kmean - part.astype(_F32)
        term = _dot(part, qh16)
        g = term if g is None else g + term
    row = lax.broadcasted_iota(jnp.int32, (nq, STACK), 0)
    row_f = row.astype(_F32)
    g = jnp.where(row < i, g, NEG)
    sel = jnp.zeros((nq, STACK), jnp.bool_)
    for r in range(MOBA_TOPK):
        top = jnp.max(g, axis=0, keepdims=True)
        idx = jnp.min(jnp.where(g == top, row_f, float(nq)), axis=0, keepdims=True)
        pick = row_f == idx
        sel = sel | (pick & (jnp.full_like(row, r) < i))
        g = jnp.where(pick, -3e38, g)
    selb = jnp.where(sel, far_ref[...], NEG)
    pad = jnp.zeros((LANES - nq, STACK), _BF16)
    qaug_ref[0:LANES] = qh16
    qaug_ref[LANES:LANES + nq] = selb.astype(_BF16)
    qaug_ref[LANES + nq:2 * LANES] = pad
    qfar_ref[0:LANES] = qh16
    qfar_ref[LANES:LANES + nq] = jnp.where(row == i - 1, NEG, selb).astype(_BF16)
    qfar_ref[LANES + nq:2 * LANES] = pad

    m_ref[...] = jnp.full(m_ref.shape, NEG, _F32)
    acc_ref[...] = jnp.zeros(acc_ref.shape, _F32)

    def key_rows(j0, nblk):
        return pl.ds(pl.multiple_of(j0 * blk, blk), nblk * blk)

    def produce(buf, s):
        s_ref, smax_ref = buf
        s_ref[0:s.shape[0]] = s
        smax_ref[...] = jnp.max(s, axis=0, keepdims=True)

    def online_update(buf, blocks):
        s_ref, smax_ref = buf
        s = s_ref[0:len(blocks) * blk]
        m_prev = m_ref[...]
        m_new = jnp.maximum(m_prev, smax_ref[...])
        alpha = jnp.exp2(m_prev - m_new)
        p = jnp.exp2(s - m_new).astype(_BF16)
        for hh in range(HEADS_PER_STEP):
            cols = slice(hh * blk, (hh + 1) * blk)
            vrows = slice(hh * VT_ROWS, (hh + 1) * VT_ROWS)
            o = _dot(vt_ref[blocks[0], vrows, :], p[0:blk, cols])
            for n_, b_ in enumerate(blocks[1:], start=1):
                o = o + _dot(vt_ref[b_, vrows, :], p[n_ * blk:(n_ + 1) * blk, cols])
            acc_ref[hh] = alpha[:, cols] * acc_ref[hh] + o
        m_ref[...] = m_new

    def near_scores():
        jp = jnp.maximum(i - 1, 0)
        own = _dot(k_ref[key_rows(i, 1), :], qh16)
        prev = _dot(jnp.concatenate([k_ref[key_rows(jp, 1), :], ohk_ref[key_rows(jp, 1), :]], axis=1),
                    qaug_ref[...])
        return jnp.concatenate([own, prev], axis=0) + bias_ref[...]

    def far_scores(hf):
        rows = key_rows(hf * HALF, HALF)
        return _dot(jnp.concatenate([k_ref[rows, :], ohk_ref[rows, :]], axis=1), qfar_ref[...])

    def far_blocks(hf):
        return tuple(hf * HALF + b_ for b_ in range(HALF))

    near = (i, jnp.maximum(i - 1, 0))
    n_far = jnp.maximum(i - 1, 0)
    n_half = lax.shift_right_logical(n_far + (HALF - 1), HALF_LOG2)
    buf_a = (sa_ref, ma_ref)
    buf_b = (sb_ref, mb_ref)

    produce(buf_a, near_scores())

    @pl.when(n_half == 0)
    def _():
        online_update(buf_a, near)

    @pl.when(n_half >= 1)
    def _():
        produce(buf_b, far_scores(0))
        online_update(buf_a, near)
        n_pairs = lax.shift_right_logical(n_half - 1, 1)

        def far_pair(hf):
            produce(buf_a, far_scores(hf + 1))
            online_update(buf_b, far_blocks(hf))
            produce(buf_b, far_scores(hf + 2))
            online_update(buf_a, far_blocks(hf + 1))

        n_quads = lax.shift_right_logical(n_pairs, 1)

        def far_step(t, carry):
            far_pair(4 * t)
            far_pair(4 * t + 2)
            return carry

        lax.fori_loop(0, n_quads, far_step, 0)

        @pl.when(n_pairs - 2 * n_quads == 1)
        def _():
            far_pair(4 * n_quads)

        hl = 2 * n_pairs

        @pl.when(n_half - hl == 2)
        def _():
            produce(buf_a, far_scores(hl + 1))
            online_update(buf_b, far_blocks(hl))
            online_update(buf_a, far_blocks(hl + 1))

        @pl.when(n_half - hl == 1)
        def _():
            online_update(buf_b, far_blocks(hl))

    o_t = jnp.concatenate(
        [acc_ref[hh, 0:HEAD_DIM, :] / acc_ref[hh, HEAD_DIM:HEAD_DIM + 1, :]
         for hh in range(HEADS_PER_STEP)], axis=0)
    yb_ref[...] = (o_t.T * jax.nn.silu(zb_ref[...])).astype(_BF16)


def _moba(qt, k, ohk, vt, ksum, zb, bias_rows, far_rows, batch, seq):
    n = k.shape[0]
    nq = seq // MOBA_BLOCK
    tile = pl.BlockSpec((MOBA_BLOCK, LANES), lambda b, hp, i: (b * nq + i, hp))
    return pl.pallas_call(
        _moba_kernel,
        grid=(batch, HEAD_PAIRS, nq),
        in_specs=[
            pl.BlockSpec((None, LANES, MOBA_BLOCK), lambda b, hp, i: (b * nq + i, hp, 0)),
            pl.BlockSpec((seq, LANES), lambda b, hp, i: (b, hp)),
            pl.BlockSpec((seq, LANES), lambda b, hp, i: (0, 0)),
            pl.BlockSpec((nq, HEADS_PER_STEP * VT_ROWS, MOBA_BLOCK), lambda b, hp, i: (b, hp, 0)),
            pl.BlockSpec((nq, LANES), lambda b, hp, i: (b, hp)),
            tile,
            pl.BlockSpec((HEADS_PER_STEP, 2, 1, 2 * MOBA_BLOCK), lambda b, hp, i: (hp, 0, 0, 0)),
            pl.BlockSpec((None, 1, STACK), lambda b, hp, i: (hp, 0, 0)),
        ],
        out_specs=tile,
        out_shape=jax.ShapeDtypeStruct((n, D_ATT), _BF16),
        scratch_shapes=[
            pltpu.VMEM((2 * LANES, STACK), _BF16),
            pltpu.VMEM((2 * LANES, STACK), _BF16),
            pltpu.VMEM((NEAR_ROWS, STACK), _F32),
            pltpu.VMEM((1, STACK), _F32),
            pltpu.VMEM((HEADS_PER_STEP, VT_ROWS, MOBA_BLOCK), _F32),
            pltpu.VMEM((HALF_ROWS, STACK), _F32),
            pltpu.VMEM((HALF_ROWS, STACK), _F32),
            pltpu.VMEM((1, STACK), _F32),
            pltpu.VMEM((1, STACK), _F32),
        ],
        compiler_params=pltpu.CompilerParams(
            dimension_semantics=("arbitrary", "arbitrary", "arbitrary"),
            vmem_limit_bytes=VMEM_LIMIT),
        name="moba",
    )(qt, k, ohk, vt, ksum, zb, bias_rows, far_rows)


def _out_proj_kernel(ya_ref, yb_ref, x_ref, w_ref, g_ref, o_ref, *, final):
    y = (x_ref[...] + _dot(ya_ref[...], w_ref[0:D_SGU, :])
         + _dot(yb_ref[...], w_ref[D_SGU:D_SGU + D_ATT, :]))
    if final:
        ms = jnp.mean(y * y, axis=-1, keepdims=True)
        y = y * lax.rsqrt(ms + NORM_EPS) * g_ref[...]
    o_ref[...] = y


def _out_proj(ya, yb, x2d, w_bf, final_g, final):
    n = x2d.shape[0]
    row = lambda i: (i, 0)
    const2 = lambda i: (0, 0)
    return pl.pallas_call(
        functools.partial(_out_proj_kernel, final=final),
        grid=(n // ROWS_OUT,),
        in_specs=[
            pl.BlockSpec((ROWS_OUT, D_SGU), row),
            pl.BlockSpec((ROWS_OUT, D_ATT), row),
            pl.BlockSpec((ROWS_OUT, D_MODEL), row),
            pl.BlockSpec((D_SGU + D_ATT, D_MODEL), const2),
            pl.BlockSpec((1, D_MODEL), const2),
        ],
        out_specs=pl.BlockSpec((ROWS_OUT, D_MODEL), row),
        out_shape=jax.ShapeDtypeStruct((n, D_MODEL), _F32),
        compiler_params=pltpu.CompilerParams(
            dimension_semantics=("arbitrary",), vmem_limit_bytes=VMEM_LIMIT),
        name="out_proj",
    )(ya, yb, x2d, w_bf, final_g)


def _rel_bucket(dist):
    n = jnp.maximum(dist, 0)
    max_exact = REL_BUCKETS // 2
    nf = jnp.maximum(n, 1).astype(_F32)
    large = max_exact + (jnp.log(nf / max_exact) / math.log(REL_MAX_DIST / max_exact)
                         * (REL_BUCKETS - max_exact)).astype(jnp.int32)
    large = jnp.minimum(large, REL_BUCKETS - 1)
    return jnp.where(n < max_exact, n, large)


def _bias_tables(rel_bias):
    table = rel_bias.T.astype(_F32) * LOG2E
    period = 2 * MOBA_BLOCK
    u = jnp.arange(period, dtype=jnp.int32)
    far = table[:, REL_BUCKETS - 1]
    own = jnp.where((u < MOBA_BLOCK)[None], table[:, _rel_bucket(u)], NEG)
    prev = table[:, _rel_bucket((u + MOBA_BLOCK) % period)] - far[:, None]
    rows = jnp.stack([own, prev], axis=1)[:, :, None, :]
    far_rows = jnp.repeat(far, MOBA_BLOCK).reshape(HEAD_PAIRS, 1, STACK)
    return rows, far_rows


def _sgu_tables(w_s, b_s):
    causal = jnp.tril(jnp.ones((SGU_CHUNK, SGU_CHUNK), dtype=bool))
    w = jnp.where(causal[None], w_s, 0.0).astype(_BF16)
    pairs = w.reshape(SGU_GROUPS // 2, 2, SGU_CHUNK, SGU_CHUNK).transpose(0, 2, 1, 3)
    pairs = pairs.reshape(SGU_GROUPS // 2, SGU_CHUNK, 2 * SGU_CHUNK)
    bs_full = jnp.repeat(b_s.T.astype(_F32), SGU_GROUP_DIM, axis=1)
    return pairs, bs_full


def kernel(x, norm_g, w_in, sgu_ln_g, sgu_ln_b, sgu_w, sgu_b, w_out, rel_bias, final_g):
    batch, seq, _ = x.shape
    depth = norm_g.shape[0]
    nq = seq // MOBA_BLOCK
    assert seq % MOBA_BLOCK == 0 and nq % HALF == 0 and nq % BF16_ROWS == 0 and nq <= LANES
    x2d = x.reshape(batch * seq, D_MODEL)
    bias_rows, far_rows = _bias_tables(rel_bias)
    ohk = (jnp.arange(seq, dtype=jnp.int32)[:, None] // MOBA_BLOCK
           == jnp.arange(LANES, dtype=jnp.int32)[None, :]).astype(_BF16)
    fg = final_g.reshape(1, D_MODEL)
    for l in range(depth):
        ws_pairs, bs_full = _sgu_tables(sgu_w[l], sgu_b[l])
        w_bf = w_in[l].astype(_BF16)
        wqvt_bf = jnp.concatenate([w_bf[:, Q_SEG * D_SGU:(Q_SEG + 1) * D_SGU],
                                   w_bf[:, V_SEG * D_SGU:(V_SEG + 1) * D_SGU]], axis=1).T
        ya, qt, k, vt, zb, ksum = _in_proj(
            x2d, norm_g[l].reshape(1, D_MODEL), w_bf, wqvt_bf,
            sgu_ln_g[l].reshape(1, D_SGU), sgu_ln_b[l].reshape(1, D_SGU), ws_pairs, bs_full)
        yb = _moba(qt, k, ohk, vt, ksum.reshape(batch * nq, D_ATT), zb,
                   bias_rows, far_rows, batch, seq)
        x2d = _out_proj(ya, yb, x2d, w_out[l].astype(_BF16), fg, final=(l == depth - 1))
    return x2d.reshape(batch, seq, D_MODEL)
```

```python
import functools
import math

import jax
import jax.numpy as jnp
from jax import lax
from jax.experimental import pallas as pl
from jax.experimental.pallas import tpu as pltpu

D_MODEL = 1024
D_SGU = 512
D_ATT = 512
SGU_GROUPS = 8
SGU_GROUP_DIM = D_SGU // SGU_GROUPS
SGU_CHUNK = 128
ATT_HEADS = 8
HEAD_DIM = D_ATT // ATT_HEADS
MOBA_BLOCK = 256
MOBA_TOPK = 3
REL_BUCKETS = 32
REL_MAX_DIST = 128
NORM_EPS = 1e-6
LN_EPS = 1e-5
NEG = -1e30
LOG2E = math.log2(math.e)
D_IN = 3 * D_SGU + 4 * D_ATT
Q_SEG, V_SEG = 3, 5

LANES = 128
BF16_ROWS = 16
HEADS_PER_STEP = LANES // HEAD_DIM
HEAD_PAIRS = ATT_HEADS // HEADS_PER_STEP
STACK = HEADS_PER_STEP * MOBA_BLOCK
VT_ROWS = HEAD_DIM + BF16_ROWS
HALF_LOG2 = 2
HALF = 1 << HALF_LOG2
HALF_ROWS = HALF * MOBA_BLOCK
NEAR_ROWS = 2 * MOBA_BLOCK
ROWS_IN = 256
ROWS_OUT = 512
VMEM_LIMIT = 48 * 1024 * 1024

_BF16 = jnp.bfloat16
_F32 = jnp.float32


def _dot(a, b):
    return jnp.dot(a, b, preferred_element_type=_F32)


def _dot_nt(a, b):
    return lax.dot_general(a, b, (((1,), (1,)), ((), ())), preferred_element_type=_F32)


def _in_proj_kernel(x_ref, g_ref, w_ref, wqvt_ref, lng_ref, lnb_ref, ws_ref, bs_ref,
                    ya_ref, qt_ref, k_ref, vt_ref, zb_ref, ksum_ref):
    x = x_ref[...]
    ms = jnp.mean(x * x, axis=-1, keepdims=True)
    h = (x * lax.rsqrt(ms + NORM_EPS) * g_ref[...]).astype(_BF16)

    def proj(seg):
        return _dot(h, w_ref[:, seg * D_SGU:(seg + 1) * D_SGU])

    gv = jax.nn.gelu(proj(1))
    mu = jnp.mean(gv, axis=-1, keepdims=True)
    var = jnp.mean(jnp.square(gv - mu), axis=-1, keepdims=True)
    vln = (gv - mu) * lax.rsqrt(var + LN_EPS) * lng_ref[...] + lnb_ref[...]

    gate = jax.nn.gelu(proj(0)) * jax.nn.silu(proj(2))

    lane = lax.broadcasted_iota(jnp.int32, (SGU_CHUNK, LANES), 1)
    low = lane < SGU_GROUP_DIM
    for c in range(ROWS_IN // SGU_CHUNK):
        rows = slice(c * SGU_CHUNK, (c + 1) * SGU_CHUNK)
        for p in range(D_SGU // LANES):
            cols = slice(p * LANES, (p + 1) * LANES)
            vp = vln[rows, cols]
            rhs = jnp.concatenate([jnp.where(low, vp, 0.0).astype(_BF16),
                                   jnp.where(low, 0.0, vp).astype(_BF16)], axis=0)
            mixed = _dot(ws_ref[p], rhs) + bs_ref[:, cols]
            ya_ref[rows, cols] = (gate[rows, cols] * mixed).astype(_BF16)

    k = proj(4)
    k_ref[...] = k.astype(_BF16)
    ksum_ref[...] = jnp.sum(k, axis=0, keepdims=True)[None]
    zb_ref[...] = proj(6)

    qvt = _dot_nt(wqvt_ref[...], h)
    qt_ref[0] = (qvt[0:D_ATT] * (HEAD_DIM ** -0.5 * LOG2E)).astype(_BF16)
    vt = qvt[D_ATT:2 * D_ATT].astype(_BF16)
    ones = jnp.ones((BF16_ROWS, ROWS_IN), _BF16)
    for hd in range(ATT_HEADS):
        vt_ref[0, hd * VT_ROWS:hd * VT_ROWS + HEAD_DIM, :] = vt[hd * HEAD_DIM:(hd + 1) * HEAD_DIM, :]
        vt_ref[0, hd * VT_ROWS + HEAD_DIM:(hd + 1) * VT_ROWS, :] = ones


def _in_proj(x2d, g, w_bf, wqvt_bf, lng, lnb, ws_pairs, bs_full):
    n = x2d.shape[0]
    nblk = n // ROWS_IN
    row = lambda i: (i, 0)
    const2 = lambda i: (0, 0)
    seg_spec = pl.BlockSpec((ROWS_IN, D_ATT), row)
    return pl.pallas_call(
        _in_proj_kernel,
        grid=(nblk,),
        in_specs=[
            pl.BlockSpec((ROWS_IN, D_MODEL), row),
            pl.BlockSpec((1, D_MODEL), const2),
            pl.BlockSpec((D_MODEL, D_IN), const2),
            pl.BlockSpec((2 * D_ATT, D_MODEL), const2),
            pl.BlockSpec((1, D_SGU), const2),
            pl.BlockSpec((1, D_SGU), const2),
            pl.BlockSpec((D_SGU // LANES, SGU_CHUNK, 2 * SGU_CHUNK), lambda i: (0, 0, 0)),
            pl.BlockSpec((SGU_CHUNK, D_SGU), const2),
        ],
        out_specs=[
            seg_spec,
            pl.BlockSpec((1, D_ATT, ROWS_IN), lambda i: (i, 0, 0)),
            seg_spec,
            pl.BlockSpec((1, ATT_HEADS * VT_ROWS, ROWS_IN), lambda i: (i, 0, 0)),
            seg_spec,
            pl.BlockSpec((1, 1, D_ATT), lambda i: (i, 0, 0)),
        ],
        out_shape=[
            jax.ShapeDtypeStruct((n, D_SGU), _BF16),
            jax.ShapeDtypeStruct((nblk, D_ATT, ROWS_IN), _BF16),
            jax.ShapeDtypeStruct((n, D_ATT), _BF16),
            jax.ShapeDtypeStruct((nblk, ATT_HEADS * VT_ROWS, ROWS_IN), _BF16),
            jax.ShapeDtypeStruct((n, D_ATT), _F32),
            jax.ShapeDtypeStruct((nblk, 1, D_ATT), _F32),
        ],
        compiler_params=pltpu.CompilerParams(
            dimension_semantics=("arbitrary",), vmem_limit_bytes=VMEM_LIMIT),
        name="in_proj",
    )(x2d, g, w_bf, wqvt_bf, lng, lnb, ws_pairs, bs_full)


def _toeplitz(row_ref, hh, t):
    wide = jnp.broadcast_to(row_ref[hh, t], (MOBA_BLOCK, 2 * MOBA_BLOCK))
    return pltpu.roll(wide, 0, 1, stride=1, stride_axis=0)[:, 0:MOBA_BLOCK]


def _moba_kernel(qt_ref, k_ref, ohk_ref, vt_ref, ksum_ref, zb_ref, brow_ref, far_ref,
                 yb_ref, qaug_ref, qfar_ref, bias_ref, m_ref, acc_ref,
                 sa_ref, sb_ref, ma_ref, mb_ref):
    i = pl.program_id(2)
    blk = MOBA_BLOCK
    nq = ksum_ref.shape[0]

    @pl.when(i == 0)
    def _():
        for hh in range(HEADS_PER_STEP):
            for t in range(2):
                bias_ref[t * blk:(t + 1) * blk, hh * blk:(hh + 1) * blk] = _toeplitz(brow_ref, hh, t)

    qf = qt_ref[...].astype(_F32)
    ch = lax.broadcasted_iota(jnp.int32, qf.shape, 0)
    qh16 = jnp.concatenate([jnp.where(ch < HEAD_DIM, qf, 0.0),
                            jnp.where(ch < HEAD_DIM, 0.0, qf)], axis=1).astype(_BF16)
    kmean = ksum_ref[...] * (1.0 / blk)
    g = None
    for _ in range(3):
        part = kmean.astype(_BF16)
        kmean = kmean - part.astype(_F32)
        term = _dot(part, qh16)
        g = term if g is None else g + term
    row = lax.broadcasted_iota(jnp.int32, (nq, STACK), 0)
    row_f = row.astype(_F32)
    g = jnp.where(row < i, g, NEG)
    sel = jnp.zeros((nq, STACK), jnp.bool_)
    for r in range(MOBA_TOPK):
        top = jnp.max(g, axis=0, keepdims=True)
        idx = jnp.min(jnp.where(g == top, row_f, float(nq)), axis=0, keepdims=True)
        pick = row_f == idx
        sel = sel | (pick & (jnp.full_like(row, r) < i))
        g = jnp.where(pick, -3e38, g)
    selb = jnp.where(sel, far_ref[...], NEG)
    pad = jnp.zeros((LANES - nq, STACK), _BF16)
    qaug_ref[0:LANES] = qh16
    qaug_ref[LANES:LANES + nq] = selb.astype(_BF16)
    qaug_ref[LANES + nq:2 * LANES] = pad
    qfar_ref[0:LANES] = qh16
    qfar_ref[LANES:LANES + nq] = jnp.where(row == i - 1, NEG, selb).astype(_BF16)
    qfar_ref[LANES + nq:2 * LANES] = pad

    m_ref[...] = jnp.full(m_ref.shape, NEG, _F32)
    acc_ref[...] = jnp.zeros(acc_ref.shape, _F32)

    def key_rows(j0, nblk):
        return pl.ds(pl.multiple_of(j0 * blk, blk), nblk * blk)

    def produce(buf, s):
        s_ref, smax_ref = buf
        s_ref[0:s.shape[0]] = s
        smax_ref[...] = jnp.max(s, axis=0, keepdims=True)

    def online_update(buf, blocks):
        s_ref, smax_ref = buf
        s = s_ref[0:len(blocks) * blk]
        m_prev = m_ref[...]
        m_new = jnp.maximum(m_prev, smax_ref[...])
        alpha = jnp.exp2(m_prev - m_new)
        p = jnp.exp2(s - m_new).astype(_BF16)
        for hh in range(HEADS_PER_STEP):
            cols = slice(hh * blk, (hh + 1) * blk)
            vrows = slice(hh * VT_ROWS, (hh + 1) * VT_ROWS)
            o = _dot(vt_ref[blocks[0], vrows, :], p[0:blk, cols])
            for n_, b_ in enumerate(blocks[1:], start=1):
                o = o + _dot(vt_ref[b_, vrows, :], p[n_ * blk:(n_ + 1) * blk, cols])
            acc_ref[hh] = alpha[:, cols] * acc_ref[hh] + o
        m_ref[...] = m_new

    def near_scores():
        jp = jnp.maximum(i - 1, 0)
        own = _dot(k_ref[key_rows(i, 1), :], qh16)
        prev = _dot(jnp.concatenate([k_ref[key_rows(jp, 1), :], ohk_ref[key_rows(jp, 1), :]], axis=1),
                    qaug_ref[...])
        return jnp.concatenate([own, prev], axis=0) + bias_ref[...]

    def far_scores(hf):
        rows = key_rows(hf * HALF, HALF)
        return _dot(jnp.concatenate([k_ref[rows, :], ohk_ref[rows, :]], axis=1), qfar_ref[...])

    def far_blocks(hf):
        return tuple(hf * HALF + b_ for b_ in range(HALF))

    near = (i, jnp.maximum(i - 1, 0))
    n_far = jnp.maximum(i - 1, 0)
    n_half = lax.shift_right_logical(n_far + (HALF - 1), HALF_LOG2)
    buf_a = (sa_ref, ma_ref)
    buf_b = (sb_ref, mb_ref)

    produce(buf_a, near_scores())
    produce(buf_b, far_scores(0))
    online_update(buf_a, near)

    @pl.when(n_half >= 1)
    def _():
        n_pairs = lax.shift_right_logical(n_half - 1, 1)

        def far_pair(hf):
            produce(buf_a, far_scores(hf + 1))
            online_update(buf_b, far_blocks(hf))
            produce(buf_b, far_scores(hf + 2))
            online_update(buf_a, far_blocks(hf + 1))

        n_quads = lax.shift_right_logical(n_pairs, 1)

        def far_step(t, carry):
            far_pair(4 * t)
            far_pair(4 * t + 2)
            return carry

        lax.fori_loop(0, n_quads, far_step, 0)

        @pl.when(n_pairs - 2 * n_quads == 1)
        def _():
            far_pair(4 * n_quads)

        hl = 2 * n_pairs

        @pl.when(n_half - hl == 2)
        def _():
            produce(buf_a, far_scores(hl + 1))
            online_update(buf_b, far_blocks(hl))
            online_update(buf_a, far_blocks(hl + 1))

        @pl.when(n_half - hl == 1)
        def _():
            online_update(buf_b, far_blocks(hl))

    o_t = jnp.concatenate(
        [acc_ref[hh, 0:HEAD_DIM, :] / acc_ref[hh, HEAD_DIM:HEAD_DIM + 1, :]
         for hh in range(HEADS_PER_STEP)], axis=0)
    yb_ref[...] = (o_t.T * jax.nn.silu(zb_ref[...])).astype(_BF16)


def _moba(qt, k, ohk, vt, ksum, zb, bias_rows, far_rows, batch, seq):
    n = k.shape[0]
    nq = seq // MOBA_BLOCK
    tile = pl.BlockSpec((MOBA_BLOCK, LANES), lambda b, hp, i: (b * nq + i, hp))
    return pl.pallas_call(
        _moba_kernel,
        grid=(batch, HEAD_PAIRS, nq),
        in_specs=[
            pl.BlockSpec((None, LANES, MOBA_BLOCK), lambda b, hp, i: (b * nq + i, hp, 0)),
            pl.BlockSpec((seq, LANES), lambda b, hp, i: (b, hp)),
            pl.BlockSpec((seq, LANES), lambda b, hp, i: (0, 0)),
            pl.BlockSpec((nq, HEADS_PER_STEP * VT_ROWS, MOBA_BLOCK), lambda b, hp, i: (b, hp, 0)),
            pl.BlockSpec((nq, LANES), lambda b, hp, i: (b, hp)),
            tile,
            pl.BlockSpec((HEADS_PER_STEP, 2, 1, 2 * MOBA_BLOCK), lambda b, hp, i: (hp, 0, 0, 0)),
            pl.BlockSpec((None, 1, STACK), lambda b, hp, i: (hp, 0, 0)),
        ],
        out_specs=tile,
        out_shape=jax.ShapeDtypeStruct((n, D_ATT), _BF16),
        scratch_shapes=[
            pltpu.VMEM((2 * LANES, STACK), _BF16),
            pltpu.VMEM((2 * LANES, STACK), _BF16),
            pltpu.VMEM((NEAR_ROWS, STACK), _F32),
            pltpu.VMEM((1, STACK), _F32),
            pltpu.VMEM((HEADS_PER_STEP, VT_ROWS, MOBA_BLOCK), _F32),
            pltpu.VMEM((HALF_ROWS, STACK), _F32),
            pltpu.VMEM((HALF_ROWS, STACK), _F32),
            pltpu.VMEM((1, STACK), _F32),
            pltpu.VMEM((1, STACK), _F32),
        ],
        compiler_params=pltpu.CompilerParams(
            dimension_semantics=("arbitrary", "arbitrary", "arbitrary"),
            vmem_limit_bytes=VMEM_LIMIT),
        name="moba",
    )(qt, k, ohk, vt, ksum, zb, bias_rows, far_rows)


def _out_proj_kernel(ya_ref, yb_ref, x_ref, w_ref, g_ref, o_ref, *, final):
    y = (x_ref[...] + _dot(ya_ref[...], w_ref[0:D_SGU, :])
         + _dot(yb_ref[...], w_ref[D_SGU:D_SGU + D_ATT, :]))
    if final:
        ms = jnp.mean(y * y, axis=-1, keepdims=True)
        y = y * lax.rsqrt(ms + NORM_EPS) * g_ref[...]
    o_ref[...] = y


def _out_proj(ya, yb, x2d, w_bf, final_g, final):
    n = x2d.shape[0]
    row = lambda i: (i, 0)
    const2 = lambda i: (0, 0)
    return pl.pallas_call(
        functools.partial(_out_proj_kernel, final=final),
        grid=(n // ROWS_OUT,),
        in_specs=[
            pl.BlockSpec((ROWS_OUT, D_SGU), row),
            pl.BlockSpec((ROWS_OUT, D_ATT), row),
            pl.BlockSpec((ROWS_OUT, D_MODEL), row),
            pl.BlockSpec((D_SGU + D_ATT, D_MODEL), const2),
            pl.BlockSpec((1, D_MODEL), const2),
        ],
        out_specs=pl.BlockSpec((ROWS_OUT, D_MODEL), row),
        out_shape=jax.ShapeDtypeStruct((n, D_MODEL), _F32),
        compiler_params=pltpu.CompilerParams(
            dimension_semantics=("arbitrary",), vmem_limit_bytes=VMEM_LIMIT),
        name="out_proj",
    )(ya, yb, x2d, w_bf, final_g)


def _rel_bucket(dist):
    n = jnp.maximum(dist, 0)
    max_exact = REL_BUCKETS // 2
    nf = jnp.maximum(n, 1).astype(_F32)
    large = max_exact + (jnp.log(nf / max_exact) / math.log(REL_MAX_DIST / max_exact)
                         * (REL_BUCKETS - max_exact)).astype(jnp.int32)
    large = jnp.minimum(large, REL_BUCKETS - 1)
    return jnp.where(n < max_exact, n, large)


def _bias_tables(rel_bias):
    table = rel_bias.T.astype(_F32) * LOG2E
    period = 2 * MOBA_BLOCK
    u = jnp.arange(period, dtype=jnp.int32)
    far = table[:, REL_BUCKETS - 1]
    own = jnp.where((u < MOBA_BLOCK)[None], table[:, _rel_bucket(u)], NEG)
    prev = table[:, _rel_bucket((u + MOBA_BLOCK) % period)] - far[:, None]
    rows = jnp.stack([own, prev], axis=1)[:, :, None, :]
    far_rows = jnp.repeat(far, MOBA_BLOCK).reshape(HEAD_PAIRS, 1, STACK)
    return rows, far_rows


def _sgu_tables(w_s, b_s):
    causal = jnp.tril(jnp.ones((SGU_CHUNK, SGU_CHUNK), dtype=bool))
    w = jnp.where(causal[None], w_s, 0.0).astype(_BF16)
    pairs = w.reshape(SGU_GROUPS // 2, 2, SGU_CHUNK, SGU_CHUNK).transpose(0, 2, 1, 3)
    pairs = pairs.reshape(SGU_GROUPS // 2, SGU_CHUNK, 2 * SGU_CHUNK)
    bs_full = jnp.repeat(b_s.T.astype(_F32), SGU_GROUP_DIM, axis=1)
    return pairs, bs_full


def kernel(x, norm_g, w_in, sgu_ln_g, sgu_ln_b, sgu_w, sgu_b, w_out, rel_bias, final_g):
    batch, seq, _ = x.shape
    depth = norm_g.shape[0]
    nq = seq // MOBA_BLOCK
    assert seq % MOBA_BLOCK == 0 and nq % HALF == 0 and nq % BF16_ROWS == 0 and nq <= LANES
    x2d = x.reshape(batch * seq, D_MODEL)
    bias_rows, far_rows = _bias_tables(rel_bias)
    ohk = (jnp.arange(seq, dtype=jnp.int32)[:, None] // MOBA_BLOCK
           == jnp.arange(LANES, dtype=jnp.int32)[None, :]).astype(_BF16)
    fg = final_g.reshape(1, D_MODEL)
    for l in range(depth):
        ws_pairs, bs_full = _sgu_tables(sgu_w[l], sgu_b[l])
        w_bf = w_in[l].astype(_BF16)
        wqvt_bf = jnp.concatenate([w_bf[:, Q_SEG * D_SGU:(Q_SEG + 1) * D_SGU],
                                   w_bf[:, V_SEG * D_SGU:(V_SEG + 1) * D_SGU]], axis=1).T
        ya, qt, k, vt, zb, ksum = _in_proj(
            x2d, norm_g[l].reshape(1, D_MODEL), w_bf, wqvt_bf,
            sgu_ln_g[l].reshape(1, D_SGU), sgu_ln_b[l].reshape(1, D_SGU), ws_pairs, bs_full)
        yb = _moba(qt, k, ohk, vt, ksum.reshape(batch * nq, D_ATT), zb,
                   bias_rows, far_rows, batch, seq)
        x2d = _out_proj(ya, yb, x2d, w_out[l].astype(_BF16), fg, final=(l == depth - 1))
    return x2d.reshape(batch, seq, D_MODEL)
```

```python
import functools
import math

import jax
import jax.numpy as jnp
from jax import lax
from jax.experimental import pallas as pl
from jax.experimental.pallas import tpu as pltpu

D_MODEL = 1024
D_SGU = 512
D_ATT = 512
SGU_GROUPS = 8
SGU_GROUP_DIM = D_SGU // SGU_GROUPS
SGU_CHUNK = 128
ATT_HEADS = 8
HEAD_DIM = D_ATT // ATT_HEADS
MOBA_BLOCK = 256
MOBA_TOPK = 3
REL_BUCKETS = 32
REL_MAX_DIST = 128
NORM_EPS = 1e-6
LN_EPS = 1e-5
NEG = -1e30
LOG2E = math.log2(math.e)
D_IN = 3 * D_SGU + 4 * D_ATT
Q_SEG, V_SEG = 3, 5

LANES = 128
BF16_ROWS = 16
HEADS_PER_STEP = LANES // HEAD_DIM
HEAD_PAIRS = ATT_HEADS // HEADS_PER_STEP
STACK = HEADS_PER_STEP * MOBA_BLOCK
VT_ROWS = HEAD_DIM + BF16_ROWS
HALF_LOG2 = 2
HALF = 1 << HALF_LOG2
HALF_ROWS = HALF * MOBA_BLOCK
NEAR_ROWS = 2 * MOBA_BLOCK
ROWS_IN = 256
ROWS_OUT = 512
VMEM_LIMIT = 48 * 1024 * 1024

_BF16 = jnp.bfloat16
_F32 = jnp.float32


def _dot(a, b):
    return jnp.dot(a, b, preferred_element_type=_F32)


def _dot_nt(a, b):
    return lax.dot_general(a, b, (((1,), (1,)), ((), ())), preferred_element_type=_F32)


def _in_proj_kernel(x_ref, g_ref, w_ref, wqvt_ref, lng_ref, lnb_ref, ws_ref, bs_ref,
                    ya_ref, qt_ref, k_ref, vt_ref, zb_ref, ksum_ref):
    x = x_ref[...]
    ms = jnp.mean(x * x, axis=-1, keepdims=True)
    h = (x * lax.rsqrt(ms + NORM_EPS) * g_ref[...]).astype(_BF16)

    def proj(seg):
        return _dot(h, w_ref[:, seg * D_SGU:(seg + 1) * D_SGU])

    gv = jax.nn.gelu(proj(1))
    mu = jnp.mean(gv, axis=-1, keepdims=True)
    var = jnp.mean(jnp.square(gv - mu), axis=-1, keepdims=True)
    vln = (gv - mu) * lax.rsqrt(var + LN_EPS) * lng_ref[...] + lnb_ref[...]

    gate = jax.nn.gelu(proj(0)) * jax.nn.silu(proj(2))

    lane = lax.broadcasted_iota(jnp.int32, (SGU_CHUNK, LANES), 1)
    low = lane < SGU_GROUP_DIM
    for c in range(ROWS_IN // SGU_CHUNK):
        rows = slice(c * SGU_CHUNK, (c + 1) * SGU_CHUNK)
        for p in range(D_SGU // LANES):
            cols = slice(p * LANES, (p + 1) * LANES)
            vp = vln[rows, cols]
            rhs = jnp.concatenate([jnp.where(low, vp, 0.0).astype(_BF16),
                                   jnp.where(low, 0.0, vp).astype(_BF16)], axis=0)
            mixed = _dot(ws_ref[p], rhs) + bs_ref[:, cols]
            ya_ref[rows, cols] = (gate[rows, cols] * mixed).astype(_BF16)

    k = proj(4)
    k_ref[...] = k.astype(_BF16)
    ksum_ref[...] = jnp.sum(k, axis=0, keepdims=True)[None]
    zb_ref[...] = proj(6)

    qvt = _dot_nt(wqvt_ref[...], h)
    qt_ref[0] = (qvt[0:D_ATT] * (HEAD_DIM ** -0.5 * LOG2E)).astype(_BF16)
    vt = qvt[D_ATT:2 * D_ATT].astype(_BF16)
    ones = jnp.ones((BF16_ROWS, ROWS_IN), _BF16)
    for hd in range(ATT_HEADS):
        vt_ref[0, hd * VT_ROWS:hd * VT_ROWS + HEAD_DIM, :] = vt[hd * HEAD_DIM:(hd + 1) * HEAD_DIM, :]
        vt_ref[0, hd * VT_ROWS + HEAD_DIM:(hd + 1) * VT_ROWS, :] = ones


def _in_proj(x2d, g, w_bf, wqvt_bf, lng, lnb, ws_pairs, bs_full):
    n = x2d.shape[0]
    nblk = n // ROWS_IN
    row = lambda i: (i, 0)
    const2 = lambda i: (0, 0)
    seg_spec = pl.BlockSpec((ROWS_IN, D_ATT), row)
    return pl.pallas_call(
        _in_proj_kernel,
        grid=(nblk,),
        in_specs=[
            pl.BlockSpec((ROWS_IN, D_MODEL), row),
            pl.BlockSpec((1, D_MODEL), const2),
            pl.BlockSpec((D_MODEL, D_IN), const2),
            pl.BlockSpec((2 * D_ATT, D_MODEL), const2),
            pl.BlockSpec((1, D_SGU), const2),
            pl.BlockSpec((1, D_SGU), const2),
            pl.BlockSpec((D_SGU // LANES, SGU_CHUNK, 2 * SGU_CHUNK), lambda i: (0, 0, 0)),
            pl.BlockSpec((SGU_CHUNK, D_SGU), const2),
        ],
        out_specs=[
            seg_spec,
            pl.BlockSpec((1, D_ATT, ROWS_IN), lambda i: (i, 0, 0)),
            seg_spec,
            pl.BlockSpec((1, ATT_HEADS * VT_ROWS, ROWS_IN), lambda i: (i, 0, 0)),
            seg_spec,
            pl.BlockSpec((1, 1, D_ATT), lambda i: (i, 0, 0)),
        ],
        out_shape=[
            jax.ShapeDtypeStruct((n, D_SGU), _BF16),
            jax.ShapeDtypeStruct((nblk, D_ATT, ROWS_IN), _BF16),
            jax.ShapeDtypeStruct((n, D_ATT), _BF16),
            jax.ShapeDtypeStruct((nblk, ATT_HEADS * VT_ROWS, ROWS_IN), _BF16),
            jax.ShapeDtypeStruct((n, D_ATT), _F32),
            jax.ShapeDtypeStruct((nblk, 1, D_ATT), _F32),
        ],
        compiler_params=pltpu.CompilerParams(
            dimension_semantics=("arbitrary",), vmem_limit_bytes=VMEM_LIMIT),
        name="in_proj",
    )(x2d, g, w_bf, wqvt_bf, lng, lnb, ws_pairs, bs_full)


def _toeplitz(row_ref, hh, t):
    wide = jnp.broadcast_to(row_ref[hh, t], (MOBA_BLOCK, 2 * MOBA_BLOCK))
    return pltpu.roll(wide, 0, 1, stride=1, stride_axis=0)[:, 0:MOBA_BLOCK]


def _moba_kernel(qt_ref, k_ref, ohk_ref, vt_ref, ksum_ref, zb_ref, brow_ref, far_ref,
                 yb_ref, qaug_ref, qfar_ref, bias_ref, m_ref, acc_ref,
                 sa_ref, sb_ref, ma_ref, mb_ref):
    i = pl.program_id(2)
    blk = MOBA_BLOCK
    nq = ksum_ref.shape[0]

    @pl.when(i == 0)
    def _():
        for hh in range(HEADS_PER_STEP):
            for t in range(2):
                bias_ref[t * blk:(t + 1) * blk, hh * blk:(hh + 1) * blk] = _toeplitz(brow_ref, hh, t)

    qf = qt_ref[...].astype(_F32)
    ch = lax.broadcasted_iota(jnp.int32, qf.shape, 0)
    qh16 = jnp.concatenate([jnp.where(ch < HEAD_DIM, qf, 0.0),
                            jnp.where(ch < HEAD_DIM, 0.0, qf)], axis=1).astype(_BF16)
    kmean = ksum_ref[...] * (1.0 / blk)
    g = None
    for _ in range(3):
        part = kmean.astype(_BF16)
        kmean = kmean - part.astype(_F32)
        term = _dot(part, qh16)
        g = term if g is None else g + term
    row = lax.broadcasted_iota(jnp.int32, (nq, STACK), 0)
    row_f = row.astype(_F32)
    g = jnp.where(row < i, g, NEG)
    sel = jnp.zeros((nq, STACK), jnp.bool_)
    for r in range(MOBA_TOPK):
        top = jnp.max(g, axis=0, keepdims=True)
        idx = jnp.min(jnp.where(g == top, row_f, float(nq)), axis=0, keepdims=True)
        pick = row_f == idx
        sel = sel | (pick & (jnp.full_like(row, r) < i))
        g = jnp.where(pick, -3e38, g)
    selb = jnp.where(sel, far_ref[...], NEG)
    pad = jnp.zeros((LANES - nq, STACK), _BF16)
    qaug_ref[0:LANES] = qh16
    qaug_ref[LANES:LANES + nq] = selb.astype(_BF16)
    qaug_ref[LANES + nq:2 * LANES] = pad
    qfar_ref[0:LANES] = qh16
    qfar_ref[LANES:LANES + nq] = jnp.where(row == i - 1, NEG, selb).astype(_BF16)
    qfar_ref[LANES + nq:2 * LANES] = pad

    m_ref[...] = jnp.full(m_ref.shape, NEG, _F32)
    acc_ref[...] = jnp.zeros(acc_ref.shape, _F32)

    def key_rows(j0, nblk):
        return pl.ds(pl.multiple_of(j0 * blk, blk), nblk * blk)

    def produce(buf, s):
        s_ref, smax_ref = buf
        s_ref[0:s.shape[0]] = s
        smax_ref[...] = jnp.max(s, axis=0, keepdims=True)

    def online_update(buf, blocks):
        s_ref, smax_ref = buf
        s = s_ref[0:len(blocks) * blk]
        m_prev = m_ref[...]
        m_new = jnp.maximum(m_prev, smax_ref[...])
        alpha = jnp.exp2(m_prev - m_new)
        p = jnp.exp2(s - m_new).astype(_BF16)
        for hh in range(HEADS_PER_STEP):
            cols = slice(hh * blk, (hh + 1) * blk)
            vrows = slice(hh * VT_ROWS, (hh + 1) * VT_ROWS)
            o = _dot(vt_ref[blocks[0], vrows, :], p[0:blk, cols])
            for n_, b_ in enumerate(blocks[1:], start=1):
                o = o + _dot(vt_ref[b_, vrows, :], p[n_ * blk:(n_ + 1) * blk, cols])
            acc_ref[hh] = alpha[:, cols] * acc_ref[hh] + o
        m_ref[...] = m_new

    def near_scores():
        jp = jnp.maximum(i - 1, 0)
        own = _dot(k_ref[key_rows(i, 1), :], qh16)
        prev = _dot(jnp.concatenate([k_ref[key_rows(jp, 1), :], ohk_ref[key_rows(jp, 1), :]], axis=1),
                    qaug_ref[...])
        return jnp.concatenate([own, prev], axis=0) + bias_ref[...]

    def far_scores(hf):
        rows = key_rows(hf * HALF, HALF)
        return _dot(jnp.concatenate([k_ref[rows, :], ohk_ref[rows, :]], axis=1), qfar_ref[...])

    def far_blocks(hf):
        return tuple(hf * HALF + b_ for b_ in range(HALF))

    def write_output():
        o_t = jnp.concatenate(
            [acc_ref[hh, 0:HEAD_DIM, :] / acc_ref[hh, HEAD_DIM:HEAD_DIM + 1, :]
             for hh in range(HEADS_PER_STEP)], axis=0)
        yb_ref[...] = (o_t.T * jax.nn.silu(zb_ref[...])).astype(_BF16)

    near = (i, jnp.maximum(i - 1, 0))
    n_far = jnp.maximum(i - 1, 0)
    n_half = lax.shift_right_logical(n_far + (HALF - 1), HALF_LOG2)
    buf_a = (sa_ref, ma_ref)
    buf_b = (sb_ref, mb_ref)

    produce(buf_a, near_scores())
    produce(buf_b, far_scores(0))
    online_update(buf_a, near)

    @pl.when(n_half >= 1)
    def _():
        n_pairs = lax.shift_right_logical(n_half - 1, 1)

        def far_pair(hf):
            produce(buf_a, far_scores(hf + 1))
            online_update(buf_b, far_blocks(hf))
            produce(buf_b, far_scores(hf + 2))
            online_update(buf_a, far_blocks(hf + 1))

        n_quads = lax.shift_right_logical(n_pairs, 1)

        def far_step(t, carry):
            far_pair(4 * t)
            far_pair(4 * t + 2)
            return carry

        lax.fori_loop(0, n_quads, far_step, 0)

        hl = 2 * n_pairs

        def finish(odd_pair, halves_left):
            @pl.when((n_pairs - 2 * n_quads == odd_pair) & (n_half - hl == halves_left))
            def _():
                if odd_pair:
                    far_pair(4 * n_quads)
                if halves_left == 2:
                    produce(buf_a, far_scores(hl + 1))
                online_update(buf_b, far_blocks(hl))
                if halves_left == 2:
                    online_update(buf_a, far_blocks(hl + 1))
                write_output()

        for odd_pair in (0, 1):
            for halves_left in (1, 2):
                finish(odd_pair, halves_left)

    @pl.when(n_half == 0)
    def _():
        write_output()


def _moba(qt, k, ohk, vt, ksum, zb, bias_rows, far_rows, batch, seq):
    n = k.shape[0]
    nq = seq // MOBA_BLOCK
    tile = pl.BlockSpec((MOBA_BLOCK, LANES), lambda b, hp, i: (b * nq + i, hp))
    return pl.pallas_call(
        _moba_kernel,
        grid=(batch, HEAD_PAIRS, nq),
        in_specs=[
            pl.BlockSpec((None, LANES, MOBA_BLOCK), lambda b, hp, i: (b * nq + i, hp, 0)),
            pl.BlockSpec((seq, LANES), lambda b, hp, i: (b, hp)),
            pl.BlockSpec((seq, LANES), lambda b, hp, i: (0, 0)),
            pl.BlockSpec((nq, HEADS_PER_STEP * VT_ROWS, MOBA_BLOCK), lambda b, hp, i: (b, hp, 0)),
            pl.BlockSpec((nq, LANES), lambda b, hp, i: (b, hp)),
            tile,
            pl.BlockSpec((HEADS_PER_STEP, 2, 1, 2 * MOBA_BLOCK), lambda b, hp, i: (hp, 0, 0, 0)),
            pl.BlockSpec((None, 1, STACK), lambda b, hp, i: (hp, 0, 0)),
        ],
        out_specs=tile,
        out_shape=jax.ShapeDtypeStruct((n, D_ATT), _BF16),
        scratch_shapes=[
            pltpu.VMEM((2 * LANES, STACK), _BF16),
            pltpu.VMEM((2 * LANES, STACK), _BF16),
            pltpu.VMEM((NEAR_ROWS, STACK), _F32),
            pltpu.VMEM((1, STACK), _F32),
            pltpu.VMEM((HEADS_PER_STEP, VT_ROWS, MOBA_BLOCK), _F32),
            pltpu.VMEM((HALF_ROWS, STACK), _F32),
            pltpu.VMEM((HALF_ROWS, STACK), _F32),
            pltpu.VMEM((1, STACK), _F32),
            pltpu.VMEM((1, STACK), _F32),
        ],
        compiler_params=pltpu.CompilerParams(
            dimension_semantics=("arbitrary", "arbitrary", "arbitrary"),
            vmem_limit_bytes=VMEM_LIMIT),
        name="moba",
    )(qt, k, ohk, vt, ksum, zb, bias_rows, far_rows)


def _out_proj_kernel(ya_ref, yb_ref, x_ref, w_ref, g_ref, o_ref, *, final):
    y = (x_ref[...] + _dot(ya_ref[...], w_ref[0:D_SGU, :])
         + _dot(yb_ref[...], w_ref[D_SGU:D_SGU + D_ATT, :]))
    if final:
        ms = jnp.mean(y * y, axis=-1, keepdims=True)
        y = y * lax.rsqrt(ms + NORM_EPS) * g_ref[...]
    o_ref[...] = y


def _out_proj(ya, yb, x2d, w_bf, final_g, final):
    n = x2d.shape[0]
    row = lambda i: (i, 0)
    const2 = lambda i: (0, 0)
    return pl.pallas_call(
        functools.partial(_out_proj_kernel, final=final),
        grid=(n // ROWS_OUT,),
        in_specs=[
            pl.BlockSpec((ROWS_OUT, D_SGU), row),
            pl.BlockSpec((ROWS_OUT, D_ATT), row),
            pl.BlockSpec((ROWS_OUT, D_MODEL), row),
            pl.BlockSpec((D_SGU + D_ATT, D_MODEL), const2),
            pl.BlockSpec((1, D_MODEL), const2),
        ],
        out_specs=pl.BlockSpec((ROWS_OUT, D_MODEL), row),
        out_shape=jax.ShapeDtypeStruct((n, D_MODEL), _F32),
        compiler_params=pltpu.CompilerParams(
            dimension_semantics=("arbitrary",), vmem_limit_bytes=VMEM_LIMIT),
        name="out_proj",
    )(ya, yb, x2d, w_bf, final_g)


def _rel_bucket(dist):
    n = jnp.maximum(dist, 0)
    max_exact = REL_BUCKETS // 2
    nf = jnp.maximum(n, 1).astype(_F32)
    large = max_exact + (jnp.log(nf / max_exact) / math.log(REL_MAX_DIST / max_exact)
                         * (REL_BUCKETS - max_exact)).astype(jnp.int32)
    large = jnp.minimum(large, REL_BUCKETS - 1)
    return jnp.where(n < max_exact, n, large)


def _bias_tables(rel_bias):
    table = rel_bias.T.astype(_F32) * LOG2E
    period = 2 * MOBA_BLOCK
    u = jnp.arange(period, dtype=jnp.int32)
    far = table[:, REL_BUCKETS - 1]
    own = jnp.where((u < MOBA_BLOCK)[None], table[:, _rel_bucket(u)], NEG)
    prev = table[:, _rel_bucket((u + MOBA_BLOCK) % period)] - far[:, None]
    rows = jnp.stack([own, prev], axis=1)[:, :, None, :]
    far_rows = jnp.repeat(far, MOBA_BLOCK).reshape(HEAD_PAIRS, 1, STACK)
    return rows, far_rows


def _sgu_tables(w_s, b_s):
    causal = jnp.tril(jnp.ones((SGU_CHUNK, SGU_CHUNK), dtype=bool))
    w = jnp.where(causal[None], w_s, 0.0).astype(_BF16)
    pairs = w.reshape(SGU_GROUPS // 2, 2, SGU_CHUNK, SGU_CHUNK).transpose(0, 2, 1, 3)
    pairs = pairs.reshape(SGU_GROUPS // 2, SGU_CHUNK, 2 * SGU_CHUNK)
    bs_full = jnp.repeat(b_s.T.astype(_F32), SGU_GROUP_DIM, axis=1)
    return pairs, bs_full


def kernel(x, norm_g, w_in, sgu_ln_g, sgu_ln_b, sgu_w, sgu_b, w_out, rel_bias, final_g):
    batch, seq, _ = x.shape
    depth = norm_g.shape[0]
    nq = seq // MOBA_BLOCK
    assert seq % MOBA_BLOCK == 0 and nq % HALF == 0 and nq % BF16_ROWS == 0 and nq <= LANES
    x2d = x.reshape(batch * seq, D_MODEL)
    bias_rows, far_rows = _bias_tables(rel_bias)
    ohk = (jnp.arange(seq, dtype=jnp.int32)[:, None] // MOBA_BLOCK
           == jnp.arange(LANES, dtype=jnp.int32)[None, :]).astype(_BF16)
    fg = final_g.reshape(1, D_MODEL)
    for l in range(depth):
        ws_pairs, bs_full = _sgu_tables(sgu_w[l], sgu_b[l])
        w_bf = w_in[l].astype(_BF16)
        wqvt_bf = jnp.concatenate([w_bf[:, Q_SEG * D_SGU:(Q_SEG + 1) * D_SGU],
                                   w_bf[:, V_SEG * D_SGU:(V_SEG + 1) * D_SGU]], axis=1).T
        ya, qt, k, vt, zb, ksum = _in_proj(
            x2d, norm_g[l].reshape(1, D_MODEL), w_bf, wqvt_bf,
            sgu_ln_g[l].reshape(1, D_SGU), sgu_ln_b[l].reshape(1, D_SGU), ws_pairs, bs_full)
        yb = _moba(qt, k, ohk, vt, ksum.reshape(batch * nq, D_ATT), zb,
                   bias_rows, far_rows, batch, seq)
        x2d = _out_proj(ya, yb, x2d, w_out[l].astype(_BF16), fg, final=(l == depth - 1))
    return x2d.reshape(batch, seq, D_MODEL)
```

```python
import functools
import math

import jax
import jax.numpy as jnp
from jax import lax
from jax.experimental import pallas as pl
from jax.experimental.pallas import tpu as pltpu

D_MODEL = 1024
D_SGU = 512
D_ATT = 512
SGU_GROUPS = 8
SGU_GROUP_DIM = D_SGU // SGU_GROUPS
SGU_CHUNK = 128
ATT_HEADS = 8
HEAD_DIM = D_ATT // ATT_HEADS
MOBA_BLOCK = 256
MOBA_TOPK = 3
REL_BUCKETS = 32
REL_MAX_DIST = 128
NORM_EPS = 1e-6
LN_EPS = 1e-5
NEG = -1e30
LOG2E = math.log2(math.e)
D_IN = 3 * D_SGU + 4 * D_ATT
Q_SEG, V_SEG = 3, 5

LANES = 128
BF16_ROWS = 16
HEADS_PER_STEP = LANES // HEAD_DIM
HEAD_PAIRS = ATT_HEADS // HEADS_PER_STEP
STACK = HEADS_PER_STEP * MOBA_BLOCK
VT_ROWS = HEAD_DIM + BF16_ROWS
HALF_LOG2 = 2
HALF = 1 << HALF_LOG2
HALF_ROWS = HALF * MOBA_BLOCK
NEAR_ROWS = 2 * MOBA_BLOCK
ROWS_IN = 256
ROWS_OUT = 512
VMEM_LIMIT = 48 * 1024 * 1024

_BF16 = jnp.bfloat16
_F32 = jnp.float32


def _dot(a, b):
    return jnp.dot(a, b, preferred_element_type=_F32)


def _dot_nt(a, b):
    return lax.dot_general(a, b, (((1,), (1,)), ((), ())), preferred_element_type=_F32)


def _in_proj_kernel(x_ref, g_ref, w_ref, wqvt_ref, lng_ref, lnb_ref, ws_ref, bs_ref,
                    ya_ref, qt_ref, k_ref, vt_ref, zb_ref, ksum_ref):
    x = x_ref[...]
    ms = jnp.mean(x * x, axis=-1, keepdims=True)
    h = (x * lax.rsqrt(ms + NORM_EPS) * g_ref[...]).astype(_BF16)

    def proj(seg):
        return _dot(h, w_ref[:, seg * D_SGU:(seg + 1) * D_SGU])

    gv = jax.nn.gelu(proj(1))
    mu = jnp.mean(gv, axis=-1, keepdims=True)
    var = jnp.mean(jnp.square(gv - mu), axis=-1, keepdims=True)
    vln = (gv - mu) * lax.rsqrt(var + LN_EPS) * lng_ref[...] + lnb_ref[...]

    gate = jax.nn.gelu(proj(0)) * jax.nn.silu(proj(2))

    lane = lax.broadcasted_iota(jnp.int32, (SGU_CHUNK, LANES), 1)
    low = lane < SGU_GROUP_DIM
    for c in range(ROWS_IN // SGU_CHUNK):
        rows = slice(c * SGU_CHUNK, (c + 1) * SGU_CHUNK)
        for p in range(D_SGU // LANES):
            cols = slice(p * LANES, (p + 1) * LANES)
            vp = vln[rows, cols]
            rhs = jnp.concatenate([jnp.where(low, vp, 0.0).astype(_BF16),
                                   jnp.where(low, 0.0, vp).astype(_BF16)], axis=0)
            mixed = _dot(ws_ref[p], rhs) + bs_ref[:, cols]
            ya_ref[rows, cols] = (gate[rows, cols] * mixed).astype(_BF16)

    k = proj(4)
    k_ref[...] = k.astype(_BF16)
    ksum_ref[...] = jnp.sum(k, axis=0, keepdims=True)[None]
    zb_ref[...] = proj(6)

    qvt = _dot_nt(wqvt_ref[...], h)
    qt_ref[0] = (qvt[0:D_ATT] * (HEAD_DIM ** -0.5 * LOG2E)).astype(_BF16)
    vt = qvt[D_ATT:2 * D_ATT].astype(_BF16)
    ones = jnp.ones((BF16_ROWS, ROWS_IN), _BF16)
    for hd in range(ATT_HEADS):
        vt_ref[0, hd * VT_ROWS:hd * VT_ROWS + HEAD_DIM, :] = vt[hd * HEAD_DIM:(hd + 1) * HEAD_DIM, :]
        vt_ref[0, hd * VT_ROWS + HEAD_DIM:(hd + 1) * VT_ROWS, :] = ones


def _in_proj(x2d, g, w_bf, wqvt_bf, lng, lnb, ws_pairs, bs_full):
    n = x2d.shape[0]
    nblk = n // ROWS_IN
    row = lambda i: (i, 0)
    const2 = lambda i: (0, 0)
    seg_spec = pl.BlockSpec((ROWS_IN, D_ATT), row)
    return pl.pallas_call(
        _in_proj_kernel,
        grid=(nblk,),
        in_specs=[
            pl.BlockSpec((ROWS_IN, D_MODEL), row),
            pl.BlockSpec((1, D_MODEL), const2),
            pl.BlockSpec((D_MODEL, D_IN), const2),
            pl.BlockSpec((2 * D_ATT, D_MODEL), const2),
            pl.BlockSpec((1, D_SGU), const2),
            pl.BlockSpec((1, D_SGU), const2),
            pl.BlockSpec((D_SGU // LANES, SGU_CHUNK, 2 * SGU_CHUNK), lambda i: (0, 0, 0)),
            pl.BlockSpec((SGU_CHUNK, D_SGU), const2),
        ],
        out_specs=[
            seg_spec,
            pl.BlockSpec((1, D_ATT, ROWS_IN), lambda i: (i, 0, 0)),
            seg_spec,
            pl.BlockSpec((1, ATT_HEADS * VT_ROWS, ROWS_IN), lambda i: (i, 0, 0)),
            seg_spec,
            pl.BlockSpec((1, 1, D_ATT), lambda i: (i, 0, 0)),
        ],
        out_shape=[
            jax.ShapeDtypeStruct((n, D_SGU), _BF16),
            jax.ShapeDtypeStruct((nblk, D_ATT, ROWS_IN), _BF16),
            jax.ShapeDtypeStruct((n, D_ATT), _BF16),
            jax.ShapeDtypeStruct((nblk, ATT_HEADS * VT_ROWS, ROWS_IN), _BF16),
            jax.ShapeDtypeStruct((n, D_ATT), _F32),
            jax.ShapeDtypeStruct((nblk, 1, D_ATT), _F32),
        ],
        compiler_params=pltpu.CompilerParams(
            dimension_semantics=("arbitrary",), vmem_limit_bytes=VMEM_LIMIT),
        name="in_proj",
    )(x2d, g, w_bf, wqvt_bf, lng, lnb, ws_pairs, bs_full)


def _toeplitz(row_ref, hh, t):
    wide = jnp.broadcast_to(row_ref[hh, t], (MOBA_BLOCK, 2 * MOBA_BLOCK))
    return pltpu.roll(wide, 0, 1, stride=1, stride_axis=0)[:, 0:MOBA_BLOCK]


def _bias_tiles_kernel(brow_ref, bias_ref):
    blk = MOBA_BLOCK
    for hh in range(HEADS_PER_STEP):
        for t in range(2):
            bias_ref[t * blk:(t + 1) * blk, hh * blk:(hh + 1) * blk] = _toeplitz(brow_ref, hh, t)


def _bias_tiles(bias_rows):
    return pl.pallas_call(
        _bias_tiles_kernel,
        grid=(HEAD_PAIRS,),
        in_specs=[pl.BlockSpec((HEADS_PER_STEP, 2, 1, 2 * MOBA_BLOCK), lambda hp: (hp, 0, 0, 0))],
        out_specs=pl.BlockSpec((None, NEAR_ROWS, STACK), lambda hp: (hp, 0, 0)),
        out_shape=jax.ShapeDtypeStruct((HEAD_PAIRS, NEAR_ROWS, STACK), _F32),
        compiler_params=pltpu.CompilerParams(dimension_semantics=("arbitrary",)),
        name="bias_tiles",
    )(bias_rows)


def _moba_kernel(qt_ref, k_ref, ohk_ref, vt_ref, ksum_ref, zb_ref, bias_ref, far_ref,
                 yb_ref, qaug_ref, qfar_ref, m_ref, acc_ref,
                 sa_ref, sb_ref, ma_ref, mb_ref):
    i = pl.program_id(2)
    blk = MOBA_BLOCK
    nq = ksum_ref.shape[0]

    qf = qt_ref[...].astype(_F32)
    ch = lax.broadcasted_iota(jnp.int32, qf.shape, 0)
    qh16 = jnp.concatenate([jnp.where(ch < HEAD_DIM, qf, 0.0),
                            jnp.where(ch < HEAD_DIM, 0.0, qf)], axis=1).astype(_BF16)
    kmean = ksum_ref[...] * (1.0 / blk)
    g = None
    for _ in range(3):
        part = kmean.astype(_BF16)
        kmean = kmean - part.astype(_F32)
        term = _dot(part, qh16)
        g = term if g is None else g + term
    row = lax.broadcasted_iota(jnp.int32, (nq, STACK), 0)
    row_f = row.astype(_F32)
    g = jnp.where(row < i, g, NEG)
    sel = jnp.zeros((nq, STACK), jnp.bool_)
    for r in range(MOBA_TOPK):
        top = jnp.max(g, axis=0, keepdims=True)
        idx = jnp.min(jnp.where(g == top, row_f, float(nq)), axis=0, keepdims=True)
        pick = row_f == idx
        sel = sel | (pick & (jnp.full_like(row, r) < i))
        g = jnp.where(pick, -3e38, g)
    selb = jnp.where(sel, far_ref[...], NEG)
    pad = jnp.zeros((LANES - nq, STACK), _BF16)
    qaug_ref[0:LANES] = qh16
    qaug_ref[LANES:LANES + nq] = selb.astype(_BF16)
    qaug_ref[LANES + nq:2 * LANES] = pad
    qfar_ref[0:LANES] = qh16
    qfar_ref[LANES:LANES + nq] = jnp.where(row == i - 1, NEG, selb).astype(_BF16)
    qfar_ref[LANES + nq:2 * LANES] = pad

    m_ref[...] = jnp.full(m_ref.shape, NEG, _F32)
    acc_ref[...] = jnp.zeros(acc_ref.shape, _F32)

    def key_rows(j0, nblk):
        return pl.ds(pl.multiple_of(j0 * blk, blk), nblk * blk)

    def produce(buf, s):
        s_ref, smax_ref = buf
        s_ref[0:s.shape[0]] = s
        smax_ref[...] = jnp.max(s, axis=0, keepdims=True)

    def online_update(buf, blocks):
        s_ref, smax_ref = buf
        s = s_ref[0:len(blocks) * blk]
        m_prev = m_ref[...]
        m_new = jnp.maximum(m_prev, smax_ref[...])
        alpha = jnp.exp2(m_prev - m_new)
        p = jnp.exp2(s - m_new).astype(_BF16)
        for hh in range(HEADS_PER_STEP):
            cols = slice(hh * blk, (hh + 1) * blk)
            vrows = slice(hh * VT_ROWS, (hh + 1) * VT_ROWS)
            o = _dot(vt_ref[blocks[0], vrows, :], p[0:blk, cols])
            for n_, b_ in enumerate(blocks[1:], start=1):
                o = o + _dot(vt_ref[b_, vrows, :], p[n_ * blk:(n_ + 1) * blk, cols])
            acc_ref[hh] = alpha[:, cols] * acc_ref[hh] + o
        m_ref[...] = m_new

    def near_scores():
        jp = jnp.maximum(i - 1, 0)
        own = _dot(k_ref[key_rows(i, 1), :], qh16)
        prev = _dot(jnp.concatenate([k_ref[key_rows(jp, 1), :], ohk_ref[key_rows(jp, 1), :]], axis=1),
                    qaug_ref[...])
        return jnp.concatenate([own, prev], axis=0) + bias_ref[...]

    def far_scores(hf):
        rows = key_rows(hf * HALF, HALF)
        return _dot(jnp.concatenate([k_ref[rows, :], ohk_ref[rows, :]], axis=1), qfar_ref[...])

    def far_blocks(hf):
        return tuple(hf * HALF + b_ for b_ in range(HALF))

    def write_output():
        o_t = jnp.concatenate(
            [acc_ref[hh, 0:HEAD_DIM, :] / acc_ref[hh, HEAD_DIM:HEAD_DIM + 1, :]
             for hh in range(HEADS_PER_STEP)], axis=0)
        yb_ref[...] = (o_t.T * jax.nn.silu(zb_ref[...])).astype(_BF16)

    near = (i, jnp.maximum(i - 1, 0))
    n_far = jnp.maximum(i - 1, 0)
    n_half = lax.shift_right_logical(n_far + (HALF - 1), HALF_LOG2)
    buf_a = (sa_ref, ma_ref)
    buf_b = (sb_ref, mb_ref)

    produce(buf_a, near_scores())
    produce(buf_b, far_scores(0))
    online_update(buf_a, near)

    @pl.when(n_half >= 1)
    def _():
        n_pairs = lax.shift_right_logical(n_half - 1, 1)

        def far_pair(hf):
            produce(buf_a, far_scores(hf + 1))
            online_update(buf_b, far_blocks(hf))
            produce(buf_b, far_scores(hf + 2))
            online_update(buf_a, far_blocks(hf + 1))

        n_quads = lax.shift_right_logical(n_pairs, 1)

        def far_step(t, carry):
            far_pair(4 * t)
            far_pair(4 * t + 2)
            return carry

        lax.fori_loop(0, n_quads, far_step, 0)

        hl = 2 * n_pairs

        def finish(odd_pair, halves_left):
            @pl.when((n_pairs - 2 * n_quads == odd_pair) & (n_half - hl == halves_left))
            def _():
                if odd_pair:
                    far_pair(4 * n_quads)
                if halves_left == 2:
                    produce(buf_a, far_scores(hl + 1))
                online_update(buf_b, far_blocks(hl))
                if halves_left == 2:
                    online_update(buf_a, far_blocks(hl + 1))
                write_output()

        for odd_pair in (0, 1):
            for halves_left in (1, 2):
                finish(odd_pair, halves_left)

    @pl.when(n_half == 0)
    def _():
        write_output()


def _moba(qt, k, ohk, vt, ksum, zb, bias_tiles, far_rows, batch, seq):
    n = k.shape[0]
    nq = seq // MOBA_BLOCK
    tile = pl.BlockSpec((MOBA_BLOCK, LANES), lambda b, hp, i: (b * nq + i, hp))
    return pl.pallas_call(
        _moba_kernel,
        grid=(batch, HEAD_PAIRS, nq),
        in_specs=[
            pl.BlockSpec((None, LANES, MOBA_BLOCK), lambda b, hp, i: (b * nq + i, hp, 0)),
            pl.BlockSpec((seq, LANES), lambda b, hp, i: (b, hp)),
            pl.BlockSpec((seq, LANES), lambda b, hp, i: (0, 0)),
            pl.BlockSpec((nq, HEADS_PER_STEP * VT_ROWS, MOBA_BLOCK), lambda b, hp, i: (b, hp, 0)),
            pl.BlockSpec((nq, LANES), lambda b, hp, i: (b, hp)),
            tile,
            pl.BlockSpec((None, NEAR_ROWS, STACK), lambda b, hp, i: (hp, 0, 0)),
            pl.BlockSpec((None, 1, STACK), lambda b, hp, i: (hp, 0, 0)),
        ],
        out_specs=tile,
        out_shape=jax.ShapeDtypeStruct((n, D_ATT), _BF16),
        scratch_shapes=[
            pltpu.VMEM((2 * LANES, STACK), _BF16),
            pltpu.VMEM((2 * LANES, STACK), _BF16),
            pltpu.VMEM((1, STACK), _F32),
            pltpu.VMEM((HEADS_PER_STEP, VT_ROWS, MOBA_BLOCK), _F32),
            pltpu.VMEM((HALF_ROWS, STACK), _F32),
            pltpu.VMEM((HALF_ROWS, STACK), _F32),
            pltpu.VMEM((1, STACK), _F32),
            pltpu.VMEM((1, STACK), _F32),
        ],
        compiler_params=pltpu.CompilerParams(
            dimension_semantics=("arbitrary", "arbitrary", "arbitrary"),
            vmem_limit_bytes=VMEM_LIMIT),
        name="moba",
    )(qt, k, ohk, vt, ksum, zb, bias_tiles, far_rows)


def _out_proj_kernel(ya_ref, yb_ref, x_ref, w_ref, g_ref, o_ref, *, final):
    y = (x_ref[...] + _dot(ya_ref[...], w_ref[0:D_SGU, :])
         + _dot(yb_ref[...], w_ref[D_SGU:D_SGU + D_ATT, :]))
    if final:
        ms = jnp.mean(y * y, axis=-1, keepdims=True)
        y = y * lax.rsqrt(ms + NORM_EPS) * g_ref[...]
    o_ref[...] = y


def _out_proj(ya, yb, x2d, w_bf, final_g, final):
    n = x2d.shape[0]
    row = lambda i: (i, 0)
    const2 = lambda i: (0, 0)
    return pl.pallas_call(
        functools.partial(_out_proj_kernel, final=final),
        grid=(n // ROWS_OUT,),
        in_specs=[
            pl.BlockSpec((ROWS_OUT, D_SGU), row),
            pl.BlockSpec((ROWS_OUT, D_ATT), row),
            pl.BlockSpec((ROWS_OUT, D_MODEL), row),
            pl.BlockSpec((D_SGU + D_ATT, D_MODEL), const2),
            pl.BlockSpec((1, D_MODEL), const2),
        ],
        out_specs=pl.BlockSpec((ROWS_OUT, D_MODEL), row),
        out_shape=jax.ShapeDtypeStruct((n, D_MODEL), _F32),
        compiler_params=pltpu.CompilerParams(
            dimension_semantics=("arbitrary",), vmem_limit_bytes=VMEM_LIMIT),
        name="out_proj",
    )(ya, yb, x2d, w_bf, final_g)


def _rel_bucket(dist):
    n = jnp.maximum(dist, 0)
    max_exact = REL_BUCKETS // 2
    nf = jnp.maximum(n, 1).astype(_F32)
    large = max_exact + (jnp.log(nf / max_exact) / math.log(REL_MAX_DIST / max_exact)
                         * (REL_BUCKETS - max_exact)).astype(jnp.int32)
    large = jnp.minimum(large, REL_BUCKETS - 1)
    return jnp.where(n < max_exact, n, large)


def _bias_tables(rel_bias):
    table = rel_bias.T.astype(_F32) * LOG2E
    period = 2 * MOBA_BLOCK
    u = jnp.arange(period, dtype=jnp.int32)
    far = table[:, REL_BUCKETS - 1]
    own = jnp.where((u < MOBA_BLOCK)[None], table[:, _rel_bucket(u)], NEG)
    prev = table[:, _rel_bucket((u + MOBA_BLOCK) % period)] - far[:, None]
    rows = jnp.stack([own, prev], axis=1)[:, :, None, :]
    far_rows = jnp.repeat(far, MOBA_BLOCK).reshape(HEAD_PAIRS, 1, STACK)
    return rows, far_rows


def _sgu_tables(w_s, b_s):
    causal = jnp.tril(jnp.ones((SGU_CHUNK, SGU_CHUNK), dtype=bool))
    w = jnp.where(causal[None], w_s, 0.0).astype(_BF16)
    pairs = w.reshape(SGU_GROUPS // 2, 2, SGU_CHUNK, SGU_CHUNK).transpose(0, 2, 1, 3)
    pairs = pairs.reshape(SGU_GROUPS // 2, SGU_CHUNK, 2 * SGU_CHUNK)
    bs_full = jnp.repeat(b_s.T.astype(_F32), SGU_GROUP_DIM, axis=1)
    return pairs, bs_full


def kernel(x, norm_g, w_in, sgu_ln_g, sgu_ln_b, sgu_w, sgu_b, w_out, rel_bias, final_g):
    batch, seq, _ = x.shape
    depth = norm_g.shape[0]
    nq = seq // MOBA_BLOCK
    assert seq % MOBA_BLOCK == 0 and nq % HALF == 0 and nq % BF16_ROWS == 0 and nq <= LANES
    x2d = x.reshape(batch * seq, D_MODEL)
    bias_rows, far_rows = _bias_tables(rel_bias)
    bias_tiles = _bias_tiles(bias_rows)
    ohk = (jnp.arange(seq, dtype=jnp.int32)[:, None] // MOBA_BLOCK
           == jnp.arange(LANES, dtype=jnp.int32)[None, :]).astype(_BF16)
    fg = final_g.reshape(1, D_MODEL)
    for l in range(depth):
        ws_pairs, bs_full = _sgu_tables(sgu_w[l], sgu_b[l])
        w_bf = w_in[l].astype(_BF16)
        wqvt_bf = jnp.concatenate([w_bf[:, Q_SEG * D_SGU:(Q_SEG + 1) * D_SGU],
                                   w_bf[:, V_SEG * D_SGU:(V_SEG + 1) * D_SGU]], axis=1).T
        ya, qt, k, vt, zb, ksum = _in_proj(
            x2d, norm_g[l].reshape(1, D_MODEL), w_bf, wqvt_bf,
            sgu_ln_g[l].reshape(1, D_SGU), sgu_ln_b[l].reshape(1, D_SGU), ws_pairs, bs_full)
        yb = _moba(qt, k, ohk, vt, ksum.reshape(batch * nq, D_ATT), zb,
                   bias_tiles, far_rows, batch, seq)
        x2d = _out_proj(ya, yb, x2d, w_out[l].astype(_BF16), fg, final=(l == depth - 1))
    return x2d.reshape(batch, seq, D_MODEL)
```

```python
import functools
import math

import jax
import jax.numpy as jnp
from jax import lax
from jax.experimental import pallas as pl
from jax.experimental.pallas import tpu as pltpu

D_MODEL = 1024
D_SGU = 512
D_ATT = 512
SGU_GROUPS = 8
SGU_GROUP_DIM = D_SGU // SGU_GROUPS
SGU_CHUNK = 128
ATT_HEADS = 8
HEAD_DIM = D_ATT // ATT_HEADS
MOBA_BLOCK = 256
MOBA_TOPK = 3
REL_BUCKETS = 32
REL_MAX_DIST = 128
NORM_EPS = 1e-6
LN_EPS = 1e-5
NEG = -1e30
LOG2E = math.log2(math.e)
D_IN = 3 * D_SGU + 4 * D_ATT
Q_SEG, V_SEG = 3, 5

LANES = 128
BF16_ROWS = 16
HEADS_PER_STEP = LANES // HEAD_DIM
HEAD_PAIRS = ATT_HEADS // HEADS_PER_STEP
STACK = HEADS_PER_STEP * MOBA_BLOCK
VT_ROWS = HEAD_DIM + BF16_ROWS
HALF_LOG2 = 2
HALF = 1 << HALF_LOG2
HALF_ROWS = HALF * MOBA_BLOCK
NEAR_ROWS = 2 * MOBA_BLOCK
ROWS_IN = 256
ROWS_OUT = 1024
VMEM_LIMIT = 48 * 1024 * 1024

_BF16 = jnp.bfloat16
_F32 = jnp.float32


def _dot(a, b):
    return jnp.dot(a, b, preferred_element_type=_F32)


def _dot_nt(a, b):
    return lax.dot_general(a, b, (((1,), (1,)), ((), ())), preferred_element_type=_F32)


def _in_proj_kernel(x_ref, g_ref, w_ref, wqvt_ref, lng_ref, lnb_ref, ws_ref, bs_ref,
                    ya_ref, qt_ref, k_ref, vt_ref, zb_ref, ksum_ref):
    x = x_ref[...]
    ms = jnp.mean(x * x, axis=-1, keepdims=True)
    h = (x * lax.rsqrt(ms + NORM_EPS) * g_ref[...]).astype(_BF16)

    def proj(seg):
        return _dot(h, w_ref[:, seg * D_SGU:(seg + 1) * D_SGU])

    gv = jax.nn.gelu(proj(1))
    mu = jnp.mean(gv, axis=-1, keepdims=True)
    var = jnp.mean(jnp.square(gv - mu), axis=-1, keepdims=True)
    vln = (gv - mu) * lax.rsqrt(var + LN_EPS) * lng_ref[...] + lnb_ref[...]

    gate = jax.nn.gelu(proj(0)) * jax.nn.silu(proj(2))

    lane = lax.broadcasted_iota(jnp.int32, (SGU_CHUNK, LANES), 1)
    low = lane < SGU_GROUP_DIM
    for c in range(ROWS_IN // SGU_CHUNK):
        rows = slice(c * SGU_CHUNK, (c + 1) * SGU_CHUNK)
        for p in range(D_SGU // LANES):
            cols = slice(p * LANES, (p + 1) * LANES)
            vp = vln[rows, cols]
            rhs = jnp.concatenate([jnp.where(low, vp, 0.0).astype(_BF16),
                                   jnp.where(low, 0.0, vp).astype(_BF16)], axis=0)
            mixed = _dot(ws_ref[p], rhs) + bs_ref[:, cols]
            ya_ref[rows, cols] = (gate[rows, cols] * mixed).astype(_BF16)

    k = proj(4)
    k_ref[...] = k.astype(_BF16)
    ksum_ref[...] = jnp.sum(k, axis=0, keepdims=True)[None]
    zb_ref[...] = proj(6)

    qvt = _dot_nt(wqvt_ref[...], h)
    qt_ref[0] = (qvt[0:D_ATT] * (HEAD_DIM ** -0.5 * LOG2E)).astype(_BF16)
    vt = qvt[D_ATT:2 * D_ATT].astype(_BF16)
    ones = jnp.ones((BF16_ROWS, ROWS_IN), _BF16)
    for hd in range(ATT_HEADS):
        vt_ref[0, hd * VT_ROWS:hd * VT_ROWS + HEAD_DIM, :] = vt[hd * HEAD_DIM:(hd + 1) * HEAD_DIM, :]
        vt_ref[0, hd * VT_ROWS + HEAD_DIM:(hd + 1) * VT_ROWS, :] = ones


def _in_proj(x2d, g, w_bf, wqvt_bf, lng, lnb, ws_pairs, bs_full):
    n = x2d.shape[0]
    nblk = n // ROWS_IN
    row = lambda i: (i, 0)
    const2 = lambda i: (0, 0)
    seg_spec = pl.BlockSpec((ROWS_IN, D_ATT), row)
    return pl.pallas_call(
        _in_proj_kernel,
        grid=(nblk,),
        in_specs=[
            pl.BlockSpec((ROWS_IN, D_MODEL), row),
            pl.BlockSpec((1, D_MODEL), const2),
            pl.BlockSpec((D_MODEL, D_IN), const2),
            pl.BlockSpec((2 * D_ATT, D_MODEL), const2),
            pl.BlockSpec((1, D_SGU), const2),
            pl.BlockSpec((1, D_SGU), const2),
            pl.BlockSpec((D_SGU // LANES, SGU_CHUNK, 2 * SGU_CHUNK), lambda i: (0, 0, 0)),
            pl.BlockSpec((SGU_CHUNK, D_SGU), const2),
        ],
        out_specs=[
            seg_spec,
            pl.BlockSpec((1, D_ATT, ROWS_IN), lambda i: (i, 0, 0)),
            seg_spec,
            pl.BlockSpec((1, ATT_HEADS * VT_ROWS, ROWS_IN), lambda i: (i, 0, 0)),
            seg_spec,
            pl.BlockSpec((1, 1, D_ATT), lambda i: (i, 0, 0)),
        ],
        out_shape=[
            jax.ShapeDtypeStruct((n, D_SGU), _BF16),
            jax.ShapeDtypeStruct((nblk, D_ATT, ROWS_IN), _BF16),
            jax.ShapeDtypeStruct((n, D_ATT), _BF16),
            jax.ShapeDtypeStruct((nblk, ATT_HEADS * VT_ROWS, ROWS_IN), _BF16),
            jax.ShapeDtypeStruct((n, D_ATT), _F32),
            jax.ShapeDtypeStruct((nblk, 1, D_ATT), _F32),
        ],
        compiler_params=pltpu.CompilerParams(
            dimension_semantics=("arbitrary",), vmem_limit_bytes=VMEM_LIMIT),
        name="in_proj",
    )(x2d, g, w_bf, wqvt_bf, lng, lnb, ws_pairs, bs_full)


def _toeplitz(row_ref, hh, t):
    wide = jnp.broadcast_to(row_ref[hh, t], (MOBA_BLOCK, 2 * MOBA_BLOCK))
    return pltpu.roll(wide, 0, 1, stride=1, stride_axis=0)[:, 0:MOBA_BLOCK]


def _moba_kernel(qt_ref, k_ref, ohk_ref, vt_ref, ksum_ref, zb_ref, brow_ref, far_ref,
                 yb_ref, qaug_ref, qfar_ref, bias_ref, m_ref, acc_ref,
                 sa_ref, sb_ref, ma_ref, mb_ref):
    i = pl.program_id(2)
    blk = MOBA_BLOCK
    nq = ksum_ref.shape[0]

    @pl.when(i == 0)
    def _():
        for hh in range(HEADS_PER_STEP):
            for t in range(2):
                bias_ref[t * blk:(t + 1) * blk, hh * blk:(hh + 1) * blk] = _toeplitz(brow_ref, hh, t)

    qf = qt_ref[...].astype(_F32)
    ch = lax.broadcasted_iota(jnp.int32, qf.shape, 0)
    qh16 = jnp.concatenate([jnp.where(ch < HEAD_DIM, qf, 0.0),
                            jnp.where(ch < HEAD_DIM, 0.0, qf)], axis=1).astype(_BF16)
    kmean = ksum_ref[...] * (1.0 / blk)
    g = None
    for _ in range(3):
        part = kmean.astype(_BF16)
        kmean = kmean - part.astype(_F32)
        term = _dot(part, qh16)
        g = term if g is None else g + term
    row = lax.broadcasted_iota(jnp.int32, (nq, STACK), 0)
    row_f = row.astype(_F32)
    g = jnp.where(row < i, g, NEG)
    sel = jnp.zeros((nq, STACK), jnp.bool_)
    for r in range(MOBA_TOPK):
        top = jnp.max(g, axis=0, keepdims=True)
        idx = jnp.min(jnp.where(g == top, row_f, float(nq)), axis=0, keepdims=True)
        pick = row_f == idx
        sel = sel | (pick & (jnp.full_like(row, r) < i))
        g = jnp.where(pick, -3e38, g)
    selb = jnp.where(sel, far_ref[...], NEG)
    pad = jnp.zeros((LANES - nq, STACK), _BF16)
    qaug_ref[0:LANES] = qh16
    qaug_ref[LANES:LANES + nq] = selb.astype(_BF16)
    qaug_ref[LANES + nq:2 * LANES] = pad
    qfar_ref[0:LANES] = qh16
    qfar_ref[LANES:LANES + nq] = jnp.where(row == i - 1, NEG, selb).astype(_BF16)
    qfar_ref[LANES + nq:2 * LANES] = pad

    m_ref[...] = jnp.full(m_ref.shape, NEG, _F32)
    acc_ref[...] = jnp.zeros(acc_ref.shape, _F32)

    def key_rows(j0, nblk):
        return pl.ds(pl.multiple_of(j0 * blk, blk), nblk * blk)

    def produce(buf, s):
        s_ref, smax_ref = buf
        s_ref[0:s.shape[0]] = s
        smax_ref[...] = jnp.max(s, axis=0, keepdims=True)

    def online_update(buf, blocks):
        s_ref, smax_ref = buf
        s = s_ref[0:len(blocks) * blk]
        m_prev = m_ref[...]
        m_new = jnp.maximum(m_prev, smax_ref[...])
        alpha = jnp.exp2(m_prev - m_new)
        p = jnp.exp2(s - m_new).astype(_BF16)
        for hh in range(HEADS_PER_STEP):
            cols = slice(hh * blk, (hh + 1) * blk)
            vrows = slice(hh * VT_ROWS, (hh + 1) * VT_ROWS)
            o = _dot(vt_ref[blocks[0], vrows, :], p[0:blk, cols])
            for n_, b_ in enumerate(blocks[1:], start=1):
                o = o + _dot(vt_ref[b_, vrows, :], p[n_ * blk:(n_ + 1) * blk, cols])
            acc_ref[hh] = alpha[:, cols] * acc_ref[hh] + o
        m_ref[...] = m_new

    def near_scores():
        jp = jnp.maximum(i - 1, 0)
        own = _dot(k_ref[key_rows(i, 1), :], qh16)
        prev = _dot(jnp.concatenate([k_ref[key_rows(jp, 1), :], ohk_ref[key_rows(jp, 1), :]], axis=1),
                    qaug_ref[...])
        return jnp.concatenate([own, prev], axis=0) + bias_ref[...]

    def far_scores(hf):
        rows = key_rows(hf * HALF, HALF)
        return _dot(jnp.concatenate([k_ref[rows, :], ohk_ref[rows, :]], axis=1), qfar_ref[...])

    def far_blocks(hf):
        return tuple(hf * HALF + b_ for b_ in range(HALF))

    def write_output():
        o_t = jnp.concatenate(
            [acc_ref[hh, 0:HEAD_DIM, :] / acc_ref[hh, HEAD_DIM:HEAD_DIM + 1, :]
             for hh in range(HEADS_PER_STEP)], axis=0)
        yb_ref[...] = (o_t.T * jax.nn.silu(zb_ref[...])).astype(_BF16)

    near = (i, jnp.maximum(i - 1, 0))
    n_far = jnp.maximum(i - 1, 0)
    n_half = lax.shift_right_logical(n_far + (HALF - 1), HALF_LOG2)
    buf_a = (sa_ref, ma_ref)
    buf_b = (sb_ref, mb_ref)

    produce(buf_a, near_scores())
    produce(buf_b, far_scores(0))
    online_update(buf_a, near)

    @pl.when(n_half >= 1)
    def _():
        n_pairs = lax.shift_right_logical(n_half - 1, 1)

        def far_pair(hf):
            produce(buf_a, far_scores(hf + 1))
            online_update(buf_b, far_blocks(hf))
            produce(buf_b, far_scores(hf + 2))
            online_update(buf_a, far_blocks(hf + 1))

        n_quads = lax.shift_right_logical(n_pairs, 1)

        def far_step(t, carry):
            far_pair(4 * t)
            far_pair(4 * t + 2)
            return carry

        lax.fori_loop(0, n_quads, far_step, 0)

        hl = 2 * n_pairs

        def finish(odd_pair, halves_left):
            @pl.when((n_pairs - 2 * n_quads == odd_pair) & (n_half - hl == halves_left))
            def _():
                if odd_pair:
                    far_pair(4 * n_quads)
                if halves_left == 2:
                    produce(buf_a, far_scores(hl + 1))
                online_update(buf_b, far_blocks(hl))
                if halves_left == 2:
                    online_update(buf_a, far_blocks(hl + 1))
                write_output()

        for odd_pair in (0, 1):
            for halves_left in (1, 2):
                finish(odd_pair, halves_left)

    @pl.when(n_half == 0)
    def _():
        write_output()


def _moba(qt, k, ohk, vt, ksum, zb, bias_rows, far_rows, batch, seq):
    n = k.shape[0]
    nq = seq // MOBA_BLOCK
    tile = pl.BlockSpec((MOBA_BLOCK, LANES), lambda b, hp, i: (b * nq + i, hp))
    return pl.pallas_call(
        _moba_kernel,
        grid=(batch, HEAD_PAIRS, nq),
        in_specs=[
            pl.BlockSpec((None, LANES, MOBA_BLOCK), lambda b, hp, i: (b * nq + i, hp, 0)),
            pl.BlockSpec((seq, LANES), lambda b, hp, i: (b, hp)),
            pl.BlockSpec((seq, LANES), lambda b, hp, i: (0, 0)),
            pl.BlockSpec((nq, HEADS_PER_STEP * VT_ROWS, MOBA_BLOCK), lambda b, hp, i: (b, hp, 0)),
            pl.BlockSpec((nq, LANES), lambda b, hp, i: (b, hp)),
            tile,
            pl.BlockSpec((HEADS_PER_STEP, 2, 1, 2 * MOBA_BLOCK), lambda b, hp, i: (hp, 0, 0, 0)),
            pl.BlockSpec((None, 1, STACK), lambda b, hp, i: (hp, 0, 0)),
        ],
        out_specs=tile,
        out_shape=jax.ShapeDtypeStruct((n, D_ATT), _BF16),
        scratch_shapes=[
            pltpu.VMEM((2 * LANES, STACK), _BF16),
            pltpu.VMEM((2 * LANES, STACK), _BF16),
            pltpu.VMEM((NEAR_ROWS, STACK), _F32),
            pltpu.VMEM((1, STACK), _F32),
            pltpu.VMEM((HEADS_PER_STEP, VT_ROWS, MOBA_BLOCK), _F32),
            pltpu.VMEM((HALF_ROWS, STACK), _F32),
            pltpu.VMEM((HALF_ROWS, STACK), _F32),
            pltpu.VMEM((1, STACK), _F32),
            pltpu.VMEM((1, STACK), _F32),
        ],
        compiler_params=pltpu.CompilerParams(
            dimension_semantics=("arbitrary", "arbitrary", "arbitrary"),
            vmem_limit_bytes=VMEM_LIMIT),
        name="moba",
    )(qt, k, ohk, vt, ksum, zb, bias_rows, far_rows)


def _out_proj_kernel(ya_ref, yb_ref, x_ref, w_ref, g_ref, o_ref, *, final):
    y = (x_ref[...] + _dot(ya_ref[...], w_ref[0:D_SGU, :])
         + _dot(yb_ref[...], w_ref[D_SGU:D_SGU + D_ATT, :]))
    if final:
        ms = jnp.mean(y * y, axis=-1, keepdims=True)
        y = y * lax.rsqrt(ms + NORM_EPS) * g_ref[...]
    o_ref[...] = y


def _out_proj(ya, yb, x2d, w_bf, final_g, final):
    n = x2d.shape[0]
    row = lambda i: (i, 0)
    const2 = lambda i: (0, 0)
    return pl.pallas_call(
        functools.partial(_out_proj_kernel, final=final),
        grid=(n // ROWS_OUT,),
        in_specs=[
            pl.BlockSpec((ROWS_OUT, D_SGU), row),
            pl.BlockSpec((ROWS_OUT, D_ATT), row),
            pl.BlockSpec((ROWS_OUT, D_MODEL), row),
            pl.BlockSpec((D_SGU + D_ATT, D_MODEL), const2),
            pl.BlockSpec((1, D_MODEL), const2),
        ],
        out_specs=pl.BlockSpec((ROWS_OUT, D_MODEL), row),
        out_shape=jax.ShapeDtypeStruct((n, D_MODEL), _F32),
        compiler_params=pltpu.CompilerParams(
            dimension_semantics=("arbitrary",), vmem_limit_bytes=VMEM_LIMIT),
        name="out_proj",
    )(ya, yb, x2d, w_bf, final_g)


def _rel_bucket(dist):
    n = jnp.maximum(dist, 0)
    max_exact = REL_BUCKETS // 2
    nf = jnp.maximum(n, 1).astype(_F32)
    large = max_exact + (jnp.log(nf / max_exact) / math.log(REL_MAX_DIST / max_exact)
                         * (REL_BUCKETS - max_exact)).astype(jnp.int32)
    large = jnp.minimum(large, REL_BUCKETS - 1)
    return jnp.where(n < max_exact, n, large)


def _bias_tables(rel_bias):
    table = rel_bias.T.astype(_F32) * LOG2E
    period = 2 * MOBA_BLOCK
    u = jnp.arange(period, dtype=jnp.int32)
    far = table[:, REL_BUCKETS - 1]
    own = jnp.where((u < MOBA_BLOCK)[None], table[:, _rel_bucket(u)], NEG)
    prev = table[:, _rel_bucket((u + MOBA_BLOCK) % period)] - far[:, None]
    rows = jnp.stack([own, prev], axis=1)[:, :, None, :]
    far_rows = jnp.repeat(far, MOBA_BLOCK).reshape(HEAD_PAIRS, 1, STACK)
    return rows, far_rows


def _sgu_tables(w_s, b_s):
    causal = jnp.tril(jnp.ones((SGU_CHUNK, SGU_CHUNK), dtype=bool))
    w = jnp.where(causal[None], w_s, 0.0).astype(_BF16)
    pairs = w.reshape(SGU_GROUPS // 2, 2, SGU_CHUNK, SGU_CHUNK).transpose(0, 2, 1, 3)
    pairs = pairs.reshape(SGU_GROUPS // 2, SGU_CHUNK, 2 * SGU_CHUNK)
    bs_full = jnp.repeat(b_s.T.astype(_F32), SGU_GROUP_DIM, axis=1)
    return pairs, bs_full


def kernel(x, norm_g, w_in, sgu_ln_g, sgu_ln_b, sgu_w, sgu_b, w_out, rel_bias, final_g):
    batch, seq, _ = x.shape
    depth = norm_g.shape[0]
    nq = seq // MOBA_BLOCK
    assert seq % MOBA_BLOCK == 0 and nq % HALF == 0 and nq % BF16_ROWS == 0 and nq <= LANES
    x2d = x.reshape(batch * seq, D_MODEL)
    bias_rows, far_rows = _bias_tables(rel_bias)
    ohk = (jnp.arange(seq, dtype=jnp.int32)[:, None] // MOBA_BLOCK
           == jnp.arange(LANES, dtype=jnp.int32)[None, :]).astype(_BF16)
    fg = final_g.reshape(1, D_MODEL)
    for l in range(depth):
        ws_pairs, bs_full = _sgu_tables(sgu_w[l], sgu_b[l])
        w_bf = w_in[l].astype(_BF16)
        wqvt_bf = jnp.concatenate([w_bf[:, Q_SEG * D_SGU:(Q_SEG + 1) * D_SGU],
                                   w_bf[:, V_SEG * D_SGU:(V_SEG + 1) * D_SGU]], axis=1).T
        ya, qt, k, vt, zb, ksum = _in_proj(
            x2d, norm_g[l].reshape(1, D_MODEL), w_bf, wqvt_bf,
            sgu_ln_g[l].reshape(1, D_SGU), sgu_ln_b[l].reshape(1, D_SGU), ws_pairs, bs_full)
        yb = _moba(qt, k, ohk, vt, ksum.reshape(batch * nq, D_ATT), zb,
                   bias_rows, far_rows, batch, seq)
        x2d = _out_proj(ya, yb, x2d, w_out[l].astype(_BF16), fg, final=(l == depth - 1))
    return x2d.reshape(batch, seq, D_MODEL)
```
